```python
import math, functools
import jax, jax.numpy as jnp
from jax import lax
import numpy as np

D_MODEL = 1024
BATCH = 1
SEQ = 16384
DEPTH = 2

GRID_W = 64
CTX_LEN = 256
Q_BLOCK = 128
ROPE_THETA = 10000.0
EPS = 1e-6
N_EVEN = (DEPTH + 1) // 2
N_ODD = DEPTH // 2

A_HEADS = 8
A_Q_LORA = 256
A_KV_LORA = 128
A_NOPE = 64
A_ROPE = 32
A_QK = A_NOPE + A_ROPE
A_V = 64
A_WIDTH = A_HEADS * A_V

B_HEADS = 4
B_D = 64
B_V = 2 * B_D
B_WIDTH = B_HEADS * B_V

E_SIZES = (A_Q_LORA, A_KV_LORA, A_ROPE, A_WIDTH,
           2 * B_HEADS * B_D, 2 * B_HEADS * B_D, B_WIDTH, B_WIDTH)
E_IN = sum(E_SIZES)
E_MIX = A_WIDTH + B_WIDTH

C_HEADS = 8
C_KV_HEADS = 2
C_HD = 128
C_WIDTH = C_HEADS * C_HD
O_SIZES = (C_WIDTH, C_KV_HEADS * C_HD, C_KV_HEADS * C_HD, C_WIDTH)
O_IN = sum(O_SIZES)

kernel_name = 'hybrid_mla_diff_gqa_prefix_trunk'


def rmsnorm(x, g):
    xf = x.astype(jnp.float32)
    y = xf * lax.rsqrt(jnp.mean(xf * xf, axis=-1, keepdims=True) + EPS)
    return (y * g.astype(jnp.float32)).astype(x.dtype)


def split_cols(p, sizes):
    return jnp.split(p, np.cumsum(sizes)[:-1].tolist(), axis=-1)


def prefix_angles(rows, cols, dim):
    n_freq = dim // 4
    inv = ROPE_THETA ** (-jnp.arange(n_freq, dtype=jnp.float32) / n_freq)
    ang = jnp.concatenate([rows[:, None] * inv, cols[:, None] * inv], axis=-1)
    return jnp.concatenate([jnp.zeros((CTX_LEN, dim // 2), jnp.float32), ang], axis=0)


def apply_rope(x, ang):
    half = x.shape[-1] // 2
    cos = jnp.cos(ang)[None, :, None, :]
    sin = jnp.sin(ang)[None, :, None, :]
    xf = x.astype(jnp.float32)
    x1, x2 = xf[..., :half], xf[..., half:]
    return jnp.concatenate([x1 * cos - x2 * sin, x2 * cos + x1 * sin], axis=-1).astype(x.dtype)


def gqa_attend(q, k, v):
    B, T, H, d = q.shape
    Hk = k.shape[2]
    qg = q.reshape(B, T, Hk, H // Hk, d)
    s = jnp.einsum('btkgd,bskd->bkgts', qg, k, preferred_element_type=jnp.float32) * (d ** -0.5)
    p = jax.nn.softmax(s, axis=-1).astype(v.dtype)
    o = jnp.einsum('bkgts,bskd->btkgd', p, v)
    return o.reshape(B, T, H, v.shape[-1])


def diff_attend(q1, q2, k1, k2, v, lam):
    scale = q1.shape[-1] ** -0.5
    p1 = jax.nn.softmax(jnp.einsum('bthd,bshd->bhts', q1, k1, preferred_element_type=jnp.float32) * scale, axis=-1)
    p2 = jax.nn.softmax(jnp.einsum('bthd,bshd->bhts', q2, k2, preferred_element_type=jnp.float32) * scale, axis=-1)
    w = (p1 - lam * p2).astype(v.dtype)
    return jnp.einsum('bhts,bshd->bthd', w, v)


def sweep_queries(attend, q_list, kv_list):
    B, N = q_list[0].shape[:2]
    nb = N // Q_BLOCK
    qb = tuple(jnp.moveaxis(q.reshape(B, nb, Q_BLOCK, *q.shape[2:]), 1, 0) for q in q_list)
    out = lax.map(lambda qs: attend(*qs, *kv_list), qb)
    return jnp.moveaxis(out, 0, 1).reshape(B, N, *out.shape[3:])


def prefix_attend(attend, q_list, kv_list, update_ctx):
    lat = sweep_queries(attend, tuple(q[:, CTX_LEN:] for q in q_list), kv_list)
    if not update_ctx:
        return lat
    ctx_o = attend(*(q[:, :CTX_LEN] for q in q_list), *(kv[:, :CTX_LEN] for kv in kv_list))
    return jnp.concatenate([ctx_o, lat], axis=1)


def modulated_input(x_ctx, x_lat, c, c_ctx, norm_g, w_mod, b_mod):
    m_l = jax.nn.silu(c) @ w_mod + b_mod
    m_c = jax.nn.silu(c_ctx) @ w_mod + b_mod
    sh_l, sc_l, g_l = jnp.split(m_l[:, None, :], 3, axis=-1)
    sh_c, sc_c, g_c = jnp.split(m_c, 3, axis=-1)
    h = jnp.concatenate([rmsnorm(x_ctx, norm_g) * (1 + sc_c) + sh_c,
                         rmsnorm(x_lat, norm_g) * (1 + sc_l) + sh_l], axis=1)
    return h, g_c, g_l


def residual_update(x_ctx, x_lat, out, g_c, g_l, update_ctx):
    N = x_lat.shape[1]
    x_lat = x_lat + g_l * out[:, out.shape[1] - N:]
    if update_ctx:
        x_ctx = x_ctx + g_c * out[:, :CTX_LEN]
    return x_ctx, x_lat


def mla_diff_layer(x_ctx, x_lat, c, c_ctx, rows, cols, lam_init, update_ctx,
                   norm_g, w_mod, b_mod, w_in, a_g_cq, a_w_uq, a_g_ckv, a_w_ukv, a_g_qn, a_g_kn,
                   b_g_qn, b_g_kn, b_lam_q1, b_lam_k1, b_lam_q2, b_lam_k2, b_g_out, w_out):
    h, g_c, g_l = modulated_input(x_ctx, x_lat, c, c_ctx, norm_g, w_mod, b_mod)
    B, L, _ = h.shape
    a_cq, a_ckv, a_kr, a_gate, b_q, b_k, b_v, b_gate = split_cols(h @ w_in, E_SIZES)

    ang_a = prefix_angles(rows, cols, A_ROPE)
    q = (rmsnorm(a_cq, a_g_cq) @ a_w_uq).reshape(B, L, A_HEADS, A_QK)
    kv = (rmsnorm(a_ckv, a_g_ckv) @ a_w_ukv).reshape(B, L, A_HEADS, A_NOPE + A_V)
    k = jnp.concatenate([kv[..., :A_NOPE],
                         jnp.broadcast_to(a_kr[:, :, None, :], (B, L, A_HEADS, A_ROPE))], axis=-1)
    q, k = rmsnorm(q, a_g_qn), rmsnorm(k, a_g_kn)
    q = jnp.concatenate([q[..., :A_NOPE], apply_rope(q[..., A_NOPE:], ang_a)], axis=-1)
    k = jnp.concatenate([k[..., :A_NOPE], apply_rope(k[..., A_NOPE:], ang_a)], axis=-1)
    o_a = prefix_attend(gqa_attend, (q,), (k, kv[..., A_NOPE:]), update_ctx)
    T = o_a.shape[1]
    y_a = o_a.reshape(B, T, A_WIDTH) * jax.nn.silu(a_gate[:, L - T:])

    ang_b = prefix_angles(rows, cols, B_D)
    bq = apply_rope(rmsnorm(b_q.reshape(B, L, 2 * B_HEADS, B_D), b_g_qn), ang_b)
    bk = apply_rope(rmsnorm(b_k.reshape(B, L, 2 * B_HEADS, B_D), b_g_kn), ang_b)
    lq1, lk1 = b_lam_q1.astype(jnp.float32), b_lam_k1.astype(jnp.float32)
    lq2, lk2 = b_lam_q2.astype(jnp.float32), b_lam_k2.astype(jnp.float32)
    lam = jnp.exp(jnp.sum(lq1 * lk1)) - jnp.exp(jnp.sum(lq2 * lk2)) + lam_init
    o_b = prefix_attend(functools.partial(diff_attend, lam=lam),
                        (bq[:, :, 0::2], bq[:, :, 1::2]),
                        (bk[:, :, 0::2], bk[:, :, 1::2], b_v.reshape(B, L, B_HEADS, B_V)), update_ctx)
    o_b = rmsnorm(o_b, b_g_out) * (1.0 - lam_init)
    y_b = o_b.reshape(B, T, B_WIDTH) * jax.nn.silu(b_gate[:, L - T:])

    out = jnp.concatenate([y_a, y_b], axis=-1) @ w_out
    return residual_update(x_ctx, x_lat, out, g_c, g_l, update_ctx)


def gqa_layer(x_ctx, x_lat, c, c_ctx, rows, cols, update_ctx,
              norm_g, w_mod, b_mod, w_in, g_qn, g_kn, w_out):
    h, g_c, g_l = modulated_input(x_ctx, x_lat, c, c_ctx, norm_g, w_mod, b_mod)
    B, L, _ = h.shape
    q, k, v, gate = split_cols(h @ w_in, O_SIZES)
    ang = prefix_angles(rows, cols, C_HD)
    q = apply_rope(rmsnorm(q.reshape(B, L, C_HEADS, C_HD), g_qn), ang)
    k = apply_rope(rmsnorm(k.reshape(B, L, C_KV_HEADS, C_HD), g_kn), ang)
    v = v.reshape(B, L, C_KV_HEADS, C_HD)
    o = prefix_attend(gqa_attend, (q,), (k, v), update_ctx)
    T = o.shape[1]
    out = (o.reshape(B, T, C_WIDTH) * jax.nn.silu(gate[:, L - T:])) @ w_out
    return residual_update(x_ctx, x_lat, out, g_c, g_l, update_ctx)


def setup_inputs(seed: int = 0) -> dict:
    key = jax.random.key(seed)
    ks = iter(jax.random.split(key, 40))

    def nrm(shape, scale):
        return jax.random.normal(next(ks), shape, jnp.float32) * scale

    def gain(shape):
        return 1.0 + nrm(shape, 0.05)

    D = D_MODEL
    NE, NO = N_EVEN, N_ODD
    inputs = {}
    inputs['x'] = nrm((BATCH, SEQ, D), 1.0)
    inputs['c'] = nrm((BATCH, D), 1.0)
    inputs['ctx'] = nrm((BATCH, CTX_LEN, D), 1.0)
    inputs['c_ctx'] = nrm((D,), 1.0)
    inputs['e_norm_g'] = gain((NE, D))
    inputs['e_w_mod'] = nrm((NE, D, 3 * D), 0.5 * D ** -0.5)
    inputs['e_b_mod'] = nrm((NE, 3 * D), 0.02)
    inputs['e_w_in'] = nrm((NE, D, E_IN), D ** -0.5)
    inputs['a_g_cq'] = gain((NE, A_Q_LORA))
    inputs['a_w_uq'] = nrm((NE, A_Q_LORA, A_HEADS * A_QK), A_Q_LORA ** -0.5)
    inputs['a_g_ckv'] = gain((NE, A_KV_LORA))
    inputs['a_w_ukv'] = nrm((NE, A_KV_LORA, A_HEADS * (A_NOPE + A_V)), A_KV_LORA ** -0.5)
    inputs['a_g_qn'] = gain((NE, A_QK))
    inputs['a_g_kn'] = gain((NE, A_QK))
    inputs['b_g_qn'] = gain((NE, B_D))
    inputs['b_g_kn'] = gain((NE, B_D))
    inputs['b_lam_q1'] = nrm((NE, B_D), 0.1)
    inputs['b_lam_k1'] = nrm((NE, B_D), 0.1)
    inputs['b_lam_q2'] = nrm((NE, B_D), 0.1)
    inputs['b_lam_k2'] = nrm((NE, B_D), 0.1)
    inputs['b_g_out'] = gain((NE, B_V))
    inputs['e_w_out'] = nrm((NE, E_MIX, D), E_MIX ** -0.5)
    inputs['o_norm_g'] = gain((NO, D))
    inputs['o_w_mod'] = nrm((NO, D, 3 * D), 0.5 * D ** -0.5)
    inputs['o_b_mod'] = nrm((NO, 3 * D), 0.02)
    inputs['o_w_in'] = nrm((NO, D, O_IN), D ** -0.5)
    inputs['c_g_qn'] = gain((NO, C_HD))
    inputs['c_g_kn'] = gain((NO, C_HD))
    inputs['o_w_out'] = nrm((NO, C_WIDTH, D), C_WIDTH ** -0.5)
    return inputs


def reference(x, c, ctx, c_ctx, e_norm_g, e_w_mod, e_b_mod, e_w_in, a_g_cq, a_w_uq, a_g_ckv, a_w_ukv,
              a_g_qn, a_g_kn, b_g_qn, b_g_kn, b_lam_q1, b_lam_k1, b_lam_q2, b_lam_k2, b_g_out, e_w_out,
              o_norm_g, o_w_mod, o_b_mod, o_w_in, c_g_qn, c_g_kn, o_w_out):
    n_tok = x.shape[1]
    ROWS = n_tok // GRID_W
    rows, cols = jnp.meshgrid(jnp.arange(ROWS, dtype=jnp.float32),
                              jnp.arange(GRID_W, dtype=jnp.float32), indexing='ij')
    rows, cols = rows.reshape(-1), cols.reshape(-1)
    x_ctx, x_lat = ctx, x
    for layer in range(DEPTH):
        i = layer // 2
        update_ctx = layer < DEPTH - 1
        if layer % 2 == 0:
            lam_init = 0.8 - 0.6 * math.exp(-0.3 * layer)
            x_ctx, x_lat = mla_diff_layer(
                x_ctx, x_lat, c, c_ctx, rows, cols, lam_init, update_ctx,
                e_norm_g[i], e_w_mod[i], e_b_mod[i], e_w_in[i], a_g_cq[i], a_w_uq[i], a_g_ckv[i], a_w_ukv[i],
                a_g_qn[i], a_g_kn[i], b_g_qn[i], b_g_kn[i], b_lam_q1[i], b_lam_k1[i], b_lam_q2[i], b_lam_k2[i],
                b_g_out[i], e_w_out[i])
        else:
            x_ctx, x_lat = gqa_layer(
                x_ctx, x_lat, c, c_ctx, rows, cols, update_ctx,
                o_norm_g[i], o_w_mod[i], o_b_mod[i], o_w_in[i], c_g_qn[i], c_g_kn[i], o_w_out[i])
    return x_lat
```

```python
import functools
import math

import jax
import jax.numpy as jnp
from jax import lax
from jax.experimental import pallas as pl
from jax.experimental.pallas import tpu as pltpu

F32 = jnp.float32
BF16 = jnp.bfloat16

GRID_W = 64
ROPE_THETA = 10000.0
EPS = 1e-6
LOG2E = math.log2(math.e)

A_HEADS, A_Q_LORA, A_KV_LORA, A_NOPE, A_ROPE, A_V = 8, 256, 128, 64, 32, 64
A_QK = A_NOPE + A_ROPE
B_HEADS, B_D = 4, 64
B_V = 2 * B_D
C_HEADS, C_KV_HEADS, C_HD = 8, 2, 128

HEAD_PAD = 128
ROW_TILE = 256
NEG_BIG = -1e30
VMEM_LIMIT_BYTES = 56 * 1024 * 1024


def _silu(v):
    return v / (1.0 + jnp.exp(-v))


def _dot(a, b):
    return jnp.dot(a, b, preferred_element_type=F32)


def _dot_nt(a, b):
    return lax.dot_general(a, b, (((1,), (1,)), ((), ())), preferred_element_type=F32)


def _rms_rows(v, g):
    ms = jnp.mean(v * v, axis=0, keepdims=True)
    return v * lax.rsqrt(ms + EPS) * g


def _rope_rows(v, cos, sin):
    half = v.shape[0] // 2
    x1, x2 = v[:half], v[half:]
    return jnp.concatenate([x1 * cos - x2 * sin, x2 * cos + x1 * sin], axis=0)


def _compiler_params(n_grid):
    return pltpu.CompilerParams(
        dimension_semantics=("arbitrary",) * n_grid,
        vmem_limit_bytes=VMEM_LIMIT_BYTES,
    )


def _mod_kernel(c_ref, w_ref, b_ref, o_ref):
    a = _silu(c_ref[...])
    w = w_ref[...]
    a_hi = a.astype(BF16)
    a_lo = (a - a_hi.astype(F32)).astype(BF16)
    w_hi = w.astype(BF16)
    w_lo = (w - w_hi.astype(F32)).astype(BF16)
    acc = _dot(a_hi, w_hi) + _dot(a_hi, w_lo) + _dot(a_lo, w_hi)
    o_ref[...] = acc + b_ref[...]


def _modulation(cvec, w_mod, b_mod):
    d, d3 = w_mod.shape
    col = 768
    return pl.pallas_call(
        _mod_kernel,
        grid=(d3 // col,),
        in_specs=[
            pl.BlockSpec((8, d), lambda j: (0, 0)),
            pl.BlockSpec((d, col), lambda j: (0, j)),
            pl.BlockSpec((1, col), lambda j: (0, j)),
        ],
        out_specs=pl.BlockSpec((8, col), lambda j: (0, j)),
        out_shape=jax.ShapeDtypeStruct((8, d3), F32),
        compiler_params=_compiler_params(1),
        name="modulation",
    )(cvec, w_mod, b_mod.reshape(1, d3))


def _modulated_rows(x_ref, mod_ref, g_ref, is_ctx):
    d = x_ref.shape[-1]
    x = x_ref[...]
    sh = jnp.where(is_ctx, mod_ref[1:2, 0:d], mod_ref[0:1, 0:d])
    sc = jnp.where(is_ctx, mod_ref[1:2, d:2 * d], mod_ref[0:1, d:2 * d])
    ms = jnp.mean(x * x, axis=-1, keepdims=True)
    y = x * lax.rsqrt(ms + EPS) * g_ref[...]
    return (y * (1.0 + sc) + sh).astype(BF16)


def _even_pre_kernel(n_lat_tiles,
                     x_ref, mod_ref, g_ref, wt_ref,
                     gcq_ref, wuq_ref, gckv_ref, wukv_ref, gqa_ref, gka_ref,
                     cosa_ref, sina_ref, gqb_ref, gkb_ref, cosb_ref, sinb_ref,
                     qta_ref, ka_ref, vta_ref, gta_ref,
                     qtb_ref, kb_ref, vtb_ref, gtb_ref):
    is_ctx = pl.program_id(0) >= n_lat_tiles
    hb = _modulated_rows(x_ref, mod_ref, g_ref, is_ctx)
    tok = hb.shape[0]

    o_cq, o_ckv, o_kr = 0, A_Q_LORA, A_Q_LORA + A_KV_LORA
    o_ga = o_kr + A_ROPE
    o_bq = o_ga + A_HEADS * A_V
    o_bk = o_bq + 2 * B_HEADS * B_D
    o_bv = o_bk + 2 * B_HEADS * B_D
    o_gb = o_bv + B_HEADS * B_V
    o_end = o_gb + B_HEADS * B_V

    def proj(lo, hi):
        return _dot_nt(wt_ref[lo:hi, :], hb)

    cq = _rms_rows(proj(o_cq, o_ckv), gcq_ref[...]).astype(BF16)
    q_all = _dot(wuq_ref[...], cq)
    cos_a, sin_a = cosa_ref[...], sina_ref[...]
    q_scale = (A_QK ** -0.5) * LOG2E
    pad_a = jnp.zeros((HEAD_PAD - A_QK, tok), F32)
    for h in range(A_HEADS):
        qh = _rms_rows(q_all[h * A_QK:(h + 1) * A_QK], gqa_ref[...])
        qh = jnp.concatenate(
            [qh[:A_NOPE], _rope_rows(qh[A_NOPE:], cos_a, sin_a), pad_a], axis=0)
        qta_ref[h] = (qh * q_scale).astype(BF16)

    ckv_kr = proj(o_ckv, o_ga)
    ckv = _rms_rows(ckv_kr[:A_KV_LORA], gckv_ref[...]).astype(BF16)
    kr = ckv_kr[A_KV_LORA:]
    kv_all = _dot(wukv_ref[...], ckv)
    for h in range(A_HEADS):
        base = h * (A_NOPE + A_V)
        kh = jnp.concatenate([kv_all[base:base + A_NOPE], kr], axis=0)
        kh = _rms_rows(kh, gka_ref[...])
        kh = jnp.concatenate(
            [kh[:A_NOPE], _rope_rows(kh[A_NOPE:], cos_a, sin_a), pad_a], axis=0)
        ka_ref[h] = jnp.transpose(kh).astype(BF16)
        vta_ref[h] = kv_all[base + A_NOPE:base + A_NOPE + A_V].astype(BF16)

    gta_ref[...] = _silu(proj(o_ga, o_bq))

    cos_b, sin_b = cosb_ref[...], sinb_ref[...]
    qb_scale = (B_D ** -0.5) * LOG2E
    pad_b = jnp.zeros((B_D, tok), F32)
    bq = proj(o_bq, o_bk)
    bk = proj(o_bk, o_bv)
    for h in range(B_HEADS):
        ks = []
        for j in range(2):
            m = 2 * h + j
            qm = _rms_rows(bq[m * B_D:(m + 1) * B_D], gqb_ref[...])
            qm = _rope_rows(qm, cos_b, sin_b) * qb_scale
            parts = [qm, pad_b] if j == 0 else [pad_b, qm]
            qtb_ref[m] = jnp.concatenate(parts, axis=0).astype(BF16)
            km = _rms_rows(bk[m * B_D:(m + 1) * B_D], gkb_ref[...])
            ks.append(_rope_rows(km, cos_b, sin_b))
        kb_ref[h] = jnp.transpose(jnp.concatenate(ks, axis=0)).astype(BF16)
    bv = proj(o_bv, o_gb)
    for h in range(B_HEADS):
        vtb_ref[h] = bv[h * B_V:(h + 1) * B_V].astype(BF16)
    gtb_ref[...] = _silu(proj(o_gb, o_end))


def _bcast_rows(g, tok):
    return jnp.broadcast_to(g.astype(F32)[:, None], (g.shape[0], tok))


def _even_pre(xcat, mod, n_lat, norm_g, w_in, a_g_cq, a_w_uq, a_g_ckv, a_w_ukv,
              a_g_qn, a_g_kn, b_g_qn, b_g_kn, rope_a, rope_b):
    seq, d = xcat.shape
    t = ROW_TILE
    n_tiles = seq // t
    e_in = w_in.shape[1]
    const2 = lambda i: (0, 0)
    tok2 = lambda i: (0, i)
    tok3 = lambda i: (0, 0, i)
    row3 = lambda i: (0, i, 0)
    in_specs = [
        pl.BlockSpec((t, d), lambda i: (i, 0)),
        pl.BlockSpec((8, 3 * d), const2),
        pl.BlockSpec((1, d), const2),
        pl.BlockSpec((e_in, d), const2),
        pl.BlockSpec((A_Q_LORA, t), const2),
        pl.BlockSpec((A_HEADS * A_QK, A_Q_LORA), const2),
        pl.BlockSpec((A_KV_LORA, t), const2),
        pl.BlockSpec((A_HEADS * (A_NOPE + A_V), A_KV_LORA), const2),
        pl.BlockSpec((A_QK, t), const2),
        pl.BlockSpec((A_QK, t), const2),
        pl.BlockSpec((A_ROPE // 2, t), tok2),
        pl.BlockSpec((A_ROPE // 2, t), tok2),
        pl.BlockSpec((B_D, t), const2),
        pl.BlockSpec((B_D, t), const2),
        pl.BlockSpec((B_D // 2, t), tok2),
        pl.BlockSpec((B_D // 2, t), tok2),
    ]
    out_shape = [
        jax.ShapeDtypeStruct((A_HEADS, HEAD_PAD, seq), BF16),
        jax.ShapeDtypeStruct((A_HEADS, seq, HEAD_PAD), BF16),
        jax.ShapeDtypeStruct((A_HEADS, A_V, seq), BF16),
        jax.ShapeDtypeStruct((A_HEADS * A_V, seq), F32),
        jax.ShapeDtypeStruct((2 * B_HEADS, HEAD_PAD, seq), BF16),
        jax.ShapeDtypeStruct((B_HEADS, seq, HEAD_PAD), BF16),
        jax.ShapeDtypeStruct((B_HEADS, B_V, seq), BF16),
        jax.ShapeDtypeStruct((B_HEADS * B_V, seq), F32),
    ]
    out_specs = [
        pl.BlockSpec((A_HEADS, HEAD_PAD, t), tok3),
        pl.BlockSpec((A_HEADS, t, HEAD_PAD), row3),
        pl.BlockSpec((A_HEADS, A_V, t), tok3),
        pl.BlockSpec((A_HEADS * A_V, t), tok2),
        pl.BlockSpec((2 * B_HEADS, HEAD_PAD, t), tok3),
        pl.BlockSpec((B_HEADS, t, HEAD_PAD), row3),
        pl.BlockSpec((B_HEADS, B_V, t), tok3),
        pl.BlockSpec((B_HEADS * B_V, t), tok2),
    ]
    return pl.pallas_call(
        functools.partial(_even_pre_kernel, n_lat // t),
        grid=(n_tiles,),
        in_specs=in_specs,
        out_specs=out_specs,
        out_shape=out_shape,
        compiler_params=_compiler_params(1),
        name="even_pre",
    )(xcat, mod, norm_g.reshape(1, d), w_in.T.astype(BF16),
      _bcast_rows(a_g_cq, t), a_w_uq.T.astype(BF16),
      _bcast_rows(a_g_ckv, t), a_w_ukv.T.astype(BF16),
      _bcast_rows(a_g_qn, t), _bcast_rows(a_g_kn, t), rope_a[0], rope_a[1],
      _bcast_rows(b_g_qn, t), _bcast_rows(b_g_kn, t), rope_b[0], rope_b[1])


def _odd_pre_kernel(n_lat_tiles,
                    x_ref, mod_ref, g_ref, wt_ref, gq_ref, gk_ref, cos_ref, sin_ref,
                    qt_ref, k_ref, vt_ref, gt_ref):
    is_ctx = pl.program_id(0) >= n_lat_tiles
    hb = _modulated_rows(x_ref, mod_ref, g_ref, is_ctx)
    cos, sin = cos_ref[...], sin_ref[...]
    o_k = C_HEADS * C_HD
    o_v = o_k + C_KV_HEADS * C_HD
    o_g = o_v + C_KV_HEADS * C_HD
    o_end = o_g + C_HEADS * C_HD
    q_scale = (C_HD ** -0.5) * LOG2E
    q = _dot_nt(wt_ref[0:o_k, :], hb)
    for h in range(C_HEADS):
        qh = _rms_rows(q[h * C_HD:(h + 1) * C_HD], gq_ref[...])
        qt_ref[h] = (_rope_rows(qh, cos, sin) * q_scale).astype(BF16)
    kv = _dot_nt(wt_ref[o_k:o_g, :], hb)
    for h in range(C_KV_HEADS):
        kh = _rms_rows(kv[h * C_HD:(h + 1) * C_HD], gk_ref[...])
        k_ref[h] = jnp.transpose(_rope_rows(kh, cos, sin)).astype(BF16)
        vt_ref[h] = kv[(C_KV_HEADS + h) * C_HD:(C_KV_HEADS + h + 1) * C_HD].astype(BF16)
    gt_ref[...] = _silu(_dot_nt(wt_ref[o_g:o_end, :], hb))


def _odd_pre(xcat, mod, n_lat, norm_g, w_in, g_qn, g_kn, rope_c):
    seq, d = xcat.shape
    t = ROW_TILE
    o_in = w_in.shape[1]
    const2 = lambda i: (0, 0)
    tok2 = lambda i: (0, i)
    tok3 = lambda i: (0, 0, i)
    in_specs = [
        pl.BlockSpec((t, d), lambda i: (i, 0)),
        pl.BlockSpec((8, 3 * d), const2),
        pl.BlockSpec((1, d), const2),
        pl.BlockSpec((o_in, d), const2),
        pl.BlockSpec((C_HD, t), const2),
        pl.BlockSpec((C_HD, t), const2),
        pl.BlockSpec((C_HD // 2, t), tok2),
        pl.BlockSpec((C_HD // 2, t), tok2),
    ]
    out_shape = [
        jax.ShapeDtypeStruct((C_HEADS, C_HD, seq), BF16),
        jax.ShapeDtypeStruct((C_KV_HEADS, seq, C_HD), BF16),
        jax.ShapeDtypeStruct((C_KV_HEADS, C_HD, seq), BF16),
        jax.ShapeDtypeStruct((C_HEADS * C_HD, seq), F32),
    ]
    out_specs = [
        pl.BlockSpec((C_HEADS, C_HD, t), tok3),
        pl.BlockSpec((C_KV_HEADS, t, C_HD), lambda i: (0, i, 0)),
        pl.BlockSpec((C_KV_HEADS, C_HD, t), tok3),
        pl.BlockSpec((C_HEADS * C_HD, t), tok2),
    ]
    return pl.pallas_call(
        functools.partial(_odd_pre_kernel, n_lat // t),
        grid=(seq // t,),
        in_specs=in_specs,
        out_specs=out_specs,
        out_shape=out_shape,
        compiler_params=_compiler_params(1),
        name="odd_pre",
    )(xcat, mod, norm_g.reshape(1, d), w_in.T.astype(BF16),
      _bcast_rows(g_qn, t), _bcast_rows(g_kn, t), rope_c[0], rope_c[1])


def _flash_streams(qts, k_ref, vt_ref, acc_ref, n_chunks, tk):
    n_s = len(qts)
    tq = qts[0].shape[1]
    for st in range(n_s):
        acc_ref[st] = jnp.zeros(acc_ref.shape[1:], F32)

    def body(c, carry):
        ms, ls = carry
        start = pl.multiple_of(c * tk, tk)
        k = k_ref[0, pl.ds(start, tk), :]
        vt = vt_ref[0, :, pl.ds(start, tk)]
        new_ms, new_ls = [], []
        for st in range(n_s):
            s = _dot(k, qts[st])
            m_new = jnp.maximum(ms[st], jnp.max(s, axis=0, keepdims=True))
            alpha = jnp.exp2(ms[st] - m_new)
            p = jnp.exp2(s - m_new)
            new_ls.append(alpha * ls[st] + jnp.sum(p, axis=0, keepdims=True))
            acc_ref[st] = alpha * acc_ref[st] + _dot(vt, p.astype(BF16))
            new_ms.append(m_new)
        return tuple(new_ms), tuple(new_ls)

    init = (tuple(jnp.full((1, tq), NEG_BIG, F32) for _ in range(n_s)),
            tuple(jnp.zeros((1, tq), F32) for _ in range(n_s)))
    _, ls = lax.fori_loop(0, n_chunks, body, init)
    return [acc_ref[st] / ls[st] for st in range(n_s)]


def _flash_gate_kernel(n_chunks, tk, qt_ref, k_ref, vt_ref, gate_ref, yt_ref, acc_ref):
    (o,) = _flash_streams([qt_ref[0]], k_ref, vt_ref, acc_ref, n_chunks, tk)
    yt_ref[...] = (o * gate_ref[...]).astype(BF16)


def _flash_diff_kernel(lam_init, n_chunks, tk,
                       q1_ref, q2_ref, k_ref, vt_ref, gate_ref, lam_ref, gout_ref,
                       yt_ref, acc_ref):
    o1, o2 = _flash_streams([q1_ref[0], q2_ref[0]], k_ref, vt_ref, acc_ref, n_chunks, tk)
    lv = lam_ref[...]
    lam = (jnp.exp(jnp.sum(lv[0:1] * lv[1:2], axis=-1, keepdims=True))
           - jnp.exp(jnp.sum(lv[2:3] * lv[3:4], axis=-1, keepdims=True)) + lam_init)
    o = o1 - lam * o2
    o = _rms_rows(o, gout_ref[...]) * (1.0 - lam_init)
    yt_ref[...] = (o * gate_ref[...]).astype(BF16)


def _flash_call(kernel_fn, n_streams, qt, k, vt, gate, extra, *, n_heads, kv_group,
                q_tiles, q_tile0, tq, k_len, k_blk0, tk, name):
    dv = vt.shape[1]
    n_chunks = k_len // tk
    q_specs = [
        pl.BlockSpec((1, HEAD_PAD, tq),
                     functools.partial(lambda s, h, i: (n_streams * h + s, 0, q_tile0 + i), s))
        for s in range(n_streams)
    ]
    in_specs = q_specs + [
        pl.BlockSpec((1, k_len, HEAD_PAD), lambda h, i: (h // kv_group, k_blk0, 0)),
        pl.BlockSpec((1, dv, k_len), lambda h, i: (h // kv_group, 0, k_blk0)),
        pl.BlockSpec((dv, tq), lambda h, i: (h, q_tile0 + i)),
    ] + [pl.BlockSpec(e.shape, lambda h, i: (0, 0)) for e in extra]
    return pl.pallas_call(
        functools.partial(kernel_fn, n_chunks, tk),
        grid=(n_heads, q_tiles),
        in_specs=in_specs,
        out_specs=pl.BlockSpec((dv, tq), lambda h, i: (h, i)),
        out_shape=jax.ShapeDtypeStruct((n_heads * dv, q_tiles * tq), BF16),
        scratch_shapes=[pltpu.VMEM((n_streams, dv, tq), F32)],
        compiler_params=_compiler_params(2),
        name=name,
    )(*([qt] * n_streams), k, vt, gate, *extra)


def _prefix_flash(kernel_fn, n_streams, qt, k, vt, gate, extra, *, n_heads, kv_group,
                  n_lat, n_ctx, update_ctx, name):
    seq = n_lat + n_ctx
    tq = 512 if n_lat % 512 == 0 else ROW_TILE
    tk = 1280 if seq % 1280 == 0 else ROW_TILE
    common = dict(n_heads=n_heads, kv_group=kv_group)
    y = _flash_call(kernel_fn, n_streams, qt, k, vt, gate, extra(tq),
                    q_tiles=n_lat // tq, q_tile0=0, tq=tq,
                    k_len=seq, k_blk0=0, tk=tk, name=name, **common)
    if not update_ctx:
        return y
    y_ctx = _flash_call(kernel_fn, n_streams, qt, k, vt, gate, extra(n_ctx),
                        q_tiles=1, q_tile0=n_lat // n_ctx, tq=n_ctx,
                        k_len=n_ctx, k_blk0=n_lat // n_ctx, tk=n_ctx,
                        name=name + "_ctx", **common)
    return jnp.concatenate([y, y_ctx], axis=1)


def _post_kernel(n_lat_tiles, n_y, *refs):
    y_refs = refs[:n_y]
    wo_ref, x_ref, mod_ref, o_ref = refs[n_y:]
    d = x_ref.shape[-1]
    is_ctx = pl.program_id(0) >= n_lat_tiles
    acc = None
    off = 0
    for y_ref in y_refs:
        w = y_ref.shape[0]
        part = _dot(wo_ref[:, off:off + w], y_ref[...])
        acc = part if acc is None else acc + part
        off += w
    gate = jnp.where(is_ctx, mod_ref[1:2, 2 * d:3 * d], mod_ref[0:1, 2 * d:3 * d])
    o_ref[...] = x_ref[...] + gate * jnp.transpose(acc)


def _post(ys, w_out, xcat, mod, n_lat, n_rows):
    d = xcat.shape[1]
    t = ROW_TILE
    width = w_out.shape[0]
    in_specs = [pl.BlockSpec((y.shape[0], t), lambda i: (0, i)) for y in ys] + [
        pl.BlockSpec((d, width), lambda i: (0, 0)),
        pl.BlockSpec((t, d), lambda i: (i, 0)),
        pl.BlockSpec((8, 3 * d), lambda i: (0, 0)),
    ]
    return pl.pallas_call(
        functools.partial(_post_kernel, n_lat // t, len(ys)),
        grid=(n_rows // t,),
        in_specs=in_specs,
        out_specs=pl.BlockSpec((t, d), lambda i: (i, 0)),
        out_shape=jax.ShapeDtypeStruct((n_rows, d), F32),
        compiler_params=_compiler_params(1),
        name="post",
    )(*ys, w_out.T.astype(BF16), xcat, mod)


def _rope_tables(n_lat, n_ctx, dim):
    rows, cols = jnp.meshgrid(jnp.arange(n_lat // GRID_W, dtype=F32),
                              jnp.arange(GRID_W, dtype=F32), indexing="ij")
    rows, cols = rows.reshape(-1), cols.reshape(-1)
    n_freq = dim // 4
    inv = ROPE_THETA ** (-jnp.arange(n_freq, dtype=F32) / n_freq)
    ang = jnp.concatenate([rows[:, None] * inv, cols[:, None] * inv], axis=-1)
    ang = jnp.concatenate([ang, jnp.zeros((n_ctx, dim // 2), F32)], axis=0)
    return jnp.transpose(jnp.cos(ang)), jnp.transpose(jnp.sin(ang))


def kernel(x, c, ctx, c_ctx, e_norm_g, e_w_mod, e_b_mod, e_w_in, a_g_cq, a_w_uq, a_g_ckv, a_w_ukv, a_g_qn, a_g_kn, b_g_qn, b_g_kn, b_lam_q1, b_lam_k1, b_lam_q2, b_lam_k2, b_g_out, e_w_out, o_norm_g, o_w_mod, o_b_mod, o_w_in, c_g_qn, c_g_kn, o_w_out):
    batch, n_lat, d = x.shape
    n_ctx = ctx.shape[1]
    assert batch == 1 and n_ctx == ROW_TILE and n_lat % ROW_TILE == 0
    depth = e_norm_g.shape[0] + o_norm_g.shape[0]

    cvec = jnp.zeros((8, d), F32).at[0].set(c[0]).at[1].set(c_ctx)
    rope_a = _rope_tables(n_lat, n_ctx, A_ROPE)
    rope_b = _rope_tables(n_lat, n_ctx, B_D)
    rope_c = _rope_tables(n_lat, n_ctx, C_HD)

    xcat = jnp.concatenate([x[0], ctx[0]], axis=0)
    for layer in range(depth):
        i = layer // 2
        update_ctx = layer < depth - 1
        n_rows = n_lat + n_ctx if update_ctx else n_lat
        pf = dict(n_lat=n_lat, n_ctx=n_ctx, update_ctx=update_ctx)
        if layer % 2 == 0:
            lam_init = 0.8 - 0.6 * math.exp(-0.3 * layer)
            mod = _modulation(cvec, e_w_mod[i], e_b_mod[i])
            qta, ka, vta, gta, qtb, kb, vtb, gtb = _even_pre(
                xcat, mod, n_lat, e_norm_g[i], e_w_in[i], a_g_cq[i], a_w_uq[i],
                a_g_ckv[i], a_w_ukv[i], a_g_qn[i], a_g_kn[i], b_g_qn[i], b_g_kn[i],
                rope_a, rope_b)
            ya = _prefix_flash(_flash_gate_kernel, 1, qta, ka, vta, gta, lambda tq: [],
                               n_heads=A_HEADS, kv_group=1, name="flash_mla", **pf)
            lam_vecs = jnp.stack([b_lam_q1[i], b_lam_k1[i], b_lam_q2[i], b_lam_k2[i]]).astype(F32)
            yb = _prefix_flash(
                functools.partial(_flash_diff_kernel, lam_init), 2, qtb, kb, vtb, gtb,
                lambda tq: [lam_vecs, _bcast_rows(b_g_out[i], tq)],
                n_heads=B_HEADS, kv_group=1, name="flash_diff", **pf)
            xcat = _post([ya, yb], e_w_out[i], xcat, mod, n_lat, n_rows)
        else:
            mod = _modulation(cvec, o_w_mod[i], o_b_mod[i])
            qtc, kc, vtc, gtc = _odd_pre(xcat, mod, n_lat, o_norm_g[i], o_w_in[i],
                                         c_g_qn[i], c_g_kn[i], rope_c)
            yc = _prefix_flash(_flash_gate_kernel, 1, qtc, kc, vtc, gtc, lambda tq: [],
                               n_heads=C_HEADS, kv_group=C_HEADS // C_KV_HEADS,
                               name="flash_gqa", **pf)
            xcat = _post([yc], o_w_out[i], xcat, mod, n_lat, n_rows)
    return xcat[:n_lat].reshape(batch, n_lat, d)
```

```python
import functools
import math

import jax
import jax.numpy as jnp
from jax import lax
from jax.experimental import pallas as pl
from jax.experimental.pallas import tpu as pltpu

F32 = jnp.float32
BF16 = jnp.bfloat16

GRID_W = 64
ROPE_THETA = 10000.0
EPS = 1e-6
LOG2E = math.log2(math.e)

A_HEADS, A_Q_LORA, A_KV_LORA, A_NOPE, A_ROPE, A_V = 8, 256, 128, 64, 32, 64
A_QK = A_NOPE + A_ROPE
B_HEADS, B_D = 4, 64
B_V = 2 * B_D
C_HEADS, C_KV_HEADS, C_HD = 8, 2, 128

HEAD_PAD = 128
ROW_TILE = 256
NEG_BIG = -1e30
VMEM_LIMIT_BYTES = 56 * 1024 * 1024


def _silu(v):
    return v / (1.0 + jnp.exp(-v))


def _dot(a, b):
    return jnp.dot(a, b, preferred_element_type=F32)


def _dot_nt(a, b):
    return lax.dot_general(a, b, (((1,), (1,)), ((), ())), preferred_element_type=F32)


def _rms_rows(v, g):
    ms = jnp.mean(v * v, axis=0, keepdims=True)
    return v * lax.rsqrt(ms + EPS) * g


def _rope_rows(v, cos, sin):
    half = v.shape[0] // 2
    x1, x2 = v[:half], v[half:]
    return jnp.concatenate([x1 * cos - x2 * sin, x2 * cos + x1 * sin], axis=0)


def _compiler_params(n_grid):
    return pltpu.CompilerParams(
        dimension_semantics=("arbitrary",) * n_grid,
        vmem_limit_bytes=VMEM_LIMIT_BYTES,
    )


def _mod_kernel(c_ref, w_ref, b_ref, o_ref):
    a = _silu(c_ref[...])
    w = w_ref[...]
    a_hi = a.astype(BF16)
    a_lo = (a - a_hi.astype(F32)).astype(BF16)
    w_hi = w.astype(BF16)
    w_lo = (w - w_hi.astype(F32)).astype(BF16)
    acc = _dot(a_hi, w_hi) + _dot(a_hi, w_lo) + _dot(a_lo, w_hi)
    o_ref[...] = acc + b_ref[...]


def _modulation(cvec, w_mod, b_mod):
    d, d3 = w_mod.shape
    col = 768
    return pl.pallas_call(
        _mod_kernel,
        grid=(d3 // col,),
        in_specs=[
            pl.BlockSpec((8, d), lambda j: (0, 0)),
            pl.BlockSpec((d, col), lambda j: (0, j)),
            pl.BlockSpec((1, col), lambda j: (0, j)),
        ],
        out_specs=pl.BlockSpec((8, col), lambda j: (0, j)),
        out_shape=jax.ShapeDtypeStruct((8, d3), F32),
        compiler_params=_compiler_params(1),
        name="modulation",
    )(cvec, w_mod, b_mod.reshape(1, d3))


def _modulated_rows(x_ref, mod_ref, g_ref, is_ctx):
    d = x_ref.shape[-1]
    x = x_ref[...]
    sh = jnp.where(is_ctx, mod_ref[1:2, 0:d], mod_ref[0:1, 0:d])
    sc = jnp.where(is_ctx, mod_ref[1:2, d:2 * d], mod_ref[0:1, d:2 * d])
    ms = jnp.mean(x * x, axis=-1, keepdims=True)
    y = x * lax.rsqrt(ms + EPS) * g_ref[...]
    return (y * (1.0 + sc) + sh).astype(BF16)


def _even_pre_kernel(n_lat_tiles,
                     x_ref, mod_ref, g_ref, wt_ref,
                     gcq_ref, wuq_ref, gckv_ref, wukv_ref, gqa_ref, gka_ref,
                     cosa_ref, sina_ref, gqb_ref, gkb_ref, cosb_ref, sinb_ref,
                     qta_ref, ka_ref, vta_ref, gta_ref,
                     qtb_ref, kb_ref, vtb_ref, gtb_ref):
    is_ctx = pl.program_id(0) >= n_lat_tiles
    hb = _modulated_rows(x_ref, mod_ref, g_ref, is_ctx)
    tok = hb.shape[0]

    o_cq, o_ckv, o_kr = 0, A_Q_LORA, A_Q_LORA + A_KV_LORA
    o_ga = o_kr + A_ROPE
    o_bq = o_ga + A_HEADS * A_V
    o_bk = o_bq + 2 * B_HEADS * B_D
    o_bv = o_bk + 2 * B_HEADS * B_D
    o_gb = o_bv + B_HEADS * B_V
    o_end = o_gb + B_HEADS * B_V

    def proj(lo, hi):
        return _dot_nt(wt_ref[lo:hi, :], hb)

    cq = _rms_rows(proj(o_cq, o_ckv), gcq_ref[...]).astype(BF16)
    q_all = _dot(wuq_ref[...], cq)
    cos_a, sin_a = cosa_ref[...], sina_ref[...]
    q_scale = (A_QK ** -0.5) * LOG2E
    pad_a = jnp.zeros((HEAD_PAD - A_QK, tok), F32)
    for h in range(A_HEADS):
        qh = _rms_rows(q_all[h * A_QK:(h + 1) * A_QK], gqa_ref[...])
        qh = jnp.concatenate(
            [qh[:A_NOPE], _rope_rows(qh[A_NOPE:], cos_a, sin_a), pad_a], axis=0)
        qta_ref[h] = (qh * q_scale).astype(BF16)

    ckv_kr = proj(o_ckv, o_ga)
    ckv = _rms_rows(ckv_kr[:A_KV_LORA], gckv_ref[...]).astype(BF16)
    kr = ckv_kr[A_KV_LORA:]
    kv_all = _dot(wukv_ref[...], ckv)
    for h in range(A_HEADS):
        base = h * (A_NOPE + A_V)
        kh = jnp.concatenate([kv_all[base:base + A_NOPE], kr], axis=0)
        kh = _rms_rows(kh, gka_ref[...])
        kh = jnp.concatenate(
            [kh[:A_NOPE], _rope_rows(kh[A_NOPE:], cos_a, sin_a), pad_a], axis=0)
        ka_ref[h] = jnp.transpose(kh).astype(BF16)
        vta_ref[h] = kv_all[base + A_NOPE:base + A_NOPE + A_V].astype(BF16)

    gta_ref[...] = _silu(proj(o_ga, o_bq))

    cos_b, sin_b = cosb_ref[...], sinb_ref[...]
    qb_scale = (B_D ** -0.5) * LOG2E
    pad_b = jnp.zeros((B_D, tok), F32)
    bq = proj(o_bq, o_bk)
    bk = proj(o_bk, o_bv)
    for h in range(B_HEADS):
        ks = []
        for j in range(2):
            m = 2 * h + j
            qm = _rms_rows(bq[m * B_D:(m + 1) * B_D], gqb_ref[...])
            qm = _rope_rows(qm, cos_b, sin_b) * qb_scale
            parts = [qm, pad_b] if j == 0 else [pad_b, qm]
            qtb_ref[m] = jnp.concatenate(parts, axis=0).astype(BF16)
            km = _rms_rows(bk[m * B_D:(m + 1) * B_D], gkb_ref[...])
            ks.append(_rope_rows(km, cos_b, sin_b))
        kb_ref[h] = jnp.transpose(jnp.concatenate(ks, axis=0)).astype(BF16)
    bv = proj(o_bv, o_gb)
    for h in range(B_HEADS):
        vtb_ref[h] = bv[h * B_V:(h + 1) * B_V].astype(BF16)
    gtb_ref[...] = _silu(proj(o_gb, o_end))


def _bcast_rows(g, tok):
    return jnp.broadcast_to(g.astype(F32)[:, None], (g.shape[0], tok))


def _even_pre(xcat, mod, n_lat, norm_g, w_in, a_g_cq, a_w_uq, a_g_ckv, a_w_ukv,
              a_g_qn, a_g_kn, b_g_qn, b_g_kn, rope_a, rope_b):
    seq, d = xcat.shape
    t = ROW_TILE
    n_tiles = seq // t
    e_in = w_in.shape[1]
    const2 = lambda i: (0, 0)
    tok2 = lambda i: (0, i)
    tok3 = lambda i: (0, 0, i)
    row3 = lambda i: (0, i, 0)
    in_specs = [
        pl.BlockSpec((t, d), lambda i: (i, 0)),
        pl.BlockSpec((8, 3 * d), const2),
        pl.BlockSpec((1, d), const2),
        pl.BlockSpec((e_in, d), const2),
        pl.BlockSpec((A_Q_LORA, t), const2),
        pl.BlockSpec((A_HEADS * A_QK, A_Q_LORA), const2),
        pl.BlockSpec((A_KV_LORA, t), const2),
        pl.BlockSpec((A_HEADS * (A_NOPE + A_V), A_KV_LORA), const2),
        pl.BlockSpec((A_QK, t), const2),
        pl.BlockSpec((A_QK, t), const2),
        pl.BlockSpec((A_ROPE // 2, t), tok2),
        pl.BlockSpec((A_ROPE // 2, t), tok2),
        pl.BlockSpec((B_D, t), const2),
        pl.BlockSpec((B_D, t), const2),
        pl.BlockSpec((B_D // 2, t), tok2),
        pl.BlockSpec((B_D // 2, t), tok2),
    ]
    out_shape = [
        jax.ShapeDtypeStruct((A_HEADS, HEAD_PAD, seq), BF16),
        jax.ShapeDtypeStruct((A_HEADS, seq, HEAD_PAD), BF16),
        jax.ShapeDtypeStruct((A_HEADS, A_V, seq), BF16),
        jax.ShapeDtypeStruct((A_HEADS * A_V, seq), F32),
        jax.ShapeDtypeStruct((2 * B_HEADS, HEAD_PAD, seq), BF16),
        jax.ShapeDtypeStruct((B_HEADS, seq, HEAD_PAD), BF16),
        jax.ShapeDtypeStruct((B_HEADS, B_V, seq), BF16),
        jax.ShapeDtypeStruct((B_HEADS * B_V, seq), F32),
    ]
    out_specs = [
        pl.BlockSpec((A_HEADS, HEAD_PAD, t), tok3),
        pl.BlockSpec((A_HEADS, t, HEAD_PAD), row3),
        pl.BlockSpec((A_HEADS, A_V, t), tok3),
        pl.BlockSpec((A_HEADS * A_V, t), tok2),
        pl.BlockSpec((2 * B_HEADS, HEAD_PAD, t), tok3),
        pl.BlockSpec((B_HEADS, t, HEAD_PAD), row3),
        pl.BlockSpec((B_HEADS, B_V, t), tok3),
        pl.BlockSpec((B_HEADS * B_V, t), tok2),
    ]
    return pl.pallas_call(
        functools.partial(_even_pre_kernel, n_lat // t),
        grid=(n_tiles,),
        in_specs=in_specs,
        out_specs=out_specs,
        out_shape=out_shape,
        compiler_params=_compiler_params(1),
        name="even_pre",
    )(xcat, mod, norm_g.reshape(1, d), w_in.T.astype(BF16),
      _bcast_rows(a_g_cq, t), a_w_uq.T.astype(BF16),
      _bcast_rows(a_g_ckv, t), a_w_ukv.T.astype(BF16),
      _bcast_rows(a_g_qn, t), _bcast_rows(a_g_kn, t), rope_a[0], rope_a[1],
      _bcast_rows(b_g_qn, t), _bcast_rows(b_g_kn, t), rope_b[0], rope_b[1])


def _odd_pre_kernel(n_lat_tiles,
                    x_ref, mod_ref, g_ref, wt_ref, gq_ref, gk_ref, cos_ref, sin_ref,
                    qt_ref, k_ref, vt_ref, gt_ref):
    is_ctx = pl.program_id(0) >= n_lat_tiles
    hb = _modulated_rows(x_ref, mod_ref, g_ref, is_ctx)
    cos, sin = cos_ref[...], sin_ref[...]
    o_k = C_HEADS * C_HD
    o_v = o_k + C_KV_HEADS * C_HD
    o_g = o_v + C_KV_HEADS * C_HD
    o_end = o_g + C_HEADS * C_HD
    q_scale = (C_HD ** -0.5) * LOG2E
    q = _dot_nt(wt_ref[0:o_k, :], hb)
    for h in range(C_HEADS):
        qh = _rms_rows(q[h * C_HD:(h + 1) * C_HD], gq_ref[...])
        qt_ref[h] = (_rope_rows(qh, cos, sin) * q_scale).astype(BF16)
    kv = _dot_nt(wt_ref[o_k:o_g, :], hb)
    for h in range(C_KV_HEADS):
        kh = _rms_rows(kv[h * C_HD:(h + 1) * C_HD], gk_ref[...])
        k_ref[h] = jnp.transpose(_rope_rows(kh, cos, sin)).astype(BF16)
        vt_ref[h] = kv[(C_KV_HEADS + h) * C_HD:(C_KV_HEADS + h + 1) * C_HD].astype(BF16)
    gt_ref[...] = _silu(_dot_nt(wt_ref[o_g:o_end, :], hb))


def _odd_pre(xcat, mod, n_lat, norm_g, w_in, g_qn, g_kn, rope_c):
    seq, d = xcat.shape
    t = ROW_TILE
    o_in = w_in.shape[1]
    const2 = lambda i: (0, 0)
    tok2 = lambda i: (0, i)
    tok3 = lambda i: (0, 0, i)
    in_specs = [
        pl.BlockSpec((t, d), lambda i: (i, 0)),
        pl.BlockSpec((8, 3 * d), const2),
        pl.BlockSpec((1, d), const2),
        pl.BlockSpec((o_in, d), const2),
        pl.BlockSpec((C_HD, t), const2),
        pl.BlockSpec((C_HD, t), const2),
        pl.BlockSpec((C_HD // 2, t), tok2),
        pl.BlockSpec((C_HD // 2, t), tok2),
    ]
    out_shape = [
        jax.ShapeDtypeStruct((C_HEADS, C_HD, seq), BF16),
        jax.ShapeDtypeStruct((C_KV_HEADS, seq, C_HD), BF16),
        jax.ShapeDtypeStruct((C_KV_HEADS, C_HD, seq), BF16),
        jax.ShapeDtypeStruct((C_HEADS * C_HD, seq), F32),
    ]
    out_specs = [
        pl.BlockSpec((C_HEADS, C_HD, t), tok3),
        pl.BlockSpec((C_KV_HEADS, t, C_HD), lambda i: (0, i, 0)),
        pl.BlockSpec((C_KV_HEADS, C_HD, t), tok3),
        pl.BlockSpec((C_HEADS * C_HD, t), tok2),
    ]
    return pl.pallas_call(
        functools.partial(_odd_pre_kernel, n_lat // t),
        grid=(seq // t,),
        in_specs=in_specs,
        out_specs=out_specs,
        out_shape=out_shape,
        compiler_params=_compiler_params(1),
        name="odd_pre",
    )(xcat, mod, norm_g.reshape(1, d), w_in.T.astype(BF16),
      _bcast_rows(g_qn, t), _bcast_rows(g_kn, t), rope_c[0], rope_c[1])


def _flash_streams(qts, k_ref, vt_ref, s_ref, acc_ref, n_chunks, tk):
    n_s = len(qts)
    tq = qts[0].shape[1]
    for st in range(n_s):
        acc_ref[st] = jnp.zeros(acc_ref.shape[1:], F32)

    def scores(slot, c):
        start = pl.multiple_of(c * tk, tk)
        k = k_ref[0, pl.ds(start, tk), :]
        for st in range(n_s):
            s_ref[st, slot] = _dot(k, qts[st])

    def consume(slot, c, carry):
        ms, ls = carry
        start = pl.multiple_of(c * tk, tk)
        vt = vt_ref[0, :, pl.ds(start, tk)]
        new_ms, new_ls = [], []
        for st in range(n_s):
            s = s_ref[st, slot]
            m_new = jnp.maximum(ms[st], jnp.max(s, axis=0, keepdims=True))
            alpha = jnp.exp2(ms[st] - m_new)
            p = jnp.exp2(s - m_new)
            new_ls.append(alpha * ls[st] + jnp.sum(p, axis=0, keepdims=True))
            acc_ref[st] = alpha * acc_ref[st] + _dot(vt, p.astype(BF16))
            new_ms.append(m_new)
        return tuple(new_ms), tuple(new_ls)

    def pair(j, carry):
        c = 2 * j
        scores(1, c + 1)
        carry = consume(0, c, carry)
        scores(0, c + 2)
        return consume(1, c + 1, carry)

    carry = (tuple(jnp.full((1, tq), NEG_BIG, F32) for _ in range(n_s)),
             tuple(jnp.zeros((1, tq), F32) for _ in range(n_s)))
    scores(0, 0)
    carry = lax.fori_loop(0, (n_chunks - 1) // 2, pair, carry)
    if n_chunks % 2 == 0:
        scores(1, n_chunks - 1)
        carry = consume(0, n_chunks - 2, carry)
        carry = consume(1, n_chunks - 1, carry)
    else:
        carry = consume(0, n_chunks - 1, carry)
    _, ls = carry
    return [acc_ref[st] / ls[st] for st in range(n_s)]


def _flash_gate_kernel(n_chunks, tk, qt_ref, k_ref, vt_ref, gate_ref, yt_ref, s_ref, acc_ref):
    (o,) = _flash_streams([qt_ref[0]], k_ref, vt_ref, s_ref, acc_ref, n_chunks, tk)
    yt_ref[...] = (o * gate_ref[...]).astype(BF16)


def _flash_diff_kernel(lam_init, n_chunks, tk,
                       q1_ref, q2_ref, k_ref, vt_ref, gate_ref, lam_ref, gout_ref,
                       yt_ref, s_ref, acc_ref):
    o1, o2 = _flash_streams([q1_ref[0], q2_ref[0]], k_ref, vt_ref, s_ref, acc_ref,
                            n_chunks, tk)
    lv = lam_ref[...]
    lam = (jnp.exp(jnp.sum(lv[0:1] * lv[1:2], axis=-1, keepdims=True))
           - jnp.exp(jnp.sum(lv[2:3] * lv[3:4], axis=-1, keepdims=True)) + lam_init)
    o = o1 - lam * o2
    o = _rms_rows(o, gout_ref[...]) * (1.0 - lam_init)
    yt_ref[...] = (o * gate_ref[...]).astype(BF16)


def _flash_call(kernel_fn, n_streams, qt, k, vt, gate, extra, *, n_heads, kv_group,
                q_tiles, q_tile0, tq, k_len, k_blk0, tk, name):
    dv = vt.shape[1]
    n_chunks = k_len // tk
    q_specs = [
        pl.BlockSpec((1, HEAD_PAD, tq),
                     functools.partial(lambda s, h, i: (n_streams * h + s, 0, q_tile0 + i), s))
        for s in range(n_streams)
    ]
    in_specs = q_specs + [
        pl.BlockSpec((1, k_len, HEAD_PAD), lambda h, i: (h // kv_group, k_blk0, 0)),
        pl.BlockSpec((1, dv, k_len), lambda h, i: (h // kv_group, 0, k_blk0)),
        pl.BlockSpec((dv, tq), lambda h, i: (h, q_tile0 + i)),
    ] + [pl.BlockSpec(e.shape, lambda h, i: (0, 0)) for e in extra]
    return pl.pallas_call(
        functools.partial(kernel_fn, n_chunks, tk),
        grid=(n_heads, q_tiles),
        in_specs=in_specs,
        out_specs=pl.BlockSpec((dv, tq), lambda h, i: (h, i)),
        out_shape=jax.ShapeDtypeStruct((n_heads * dv, q_tiles * tq), BF16),
        scratch_shapes=[pltpu.VMEM((n_streams, 2, tk, tq), F32),
                        pltpu.VMEM((n_streams, dv, tq), F32)],
        compiler_params=_compiler_params(2),
        name=name,
    )(*([qt] * n_streams), k, vt, gate, *extra)


def _prefix_flash(kernel_fn, n_streams, qt, k, vt, gate, extra, *, n_heads, kv_group,
                  n_lat, n_ctx, update_ctx, name):
    seq = n_lat + n_ctx
    tq = 512 if n_lat % 512 == 0 else ROW_TILE
    tk = 1280 if seq % 1280 == 0 else ROW_TILE
    common = dict(n_heads=n_heads, kv_group=kv_group)
    y = _flash_call(kernel_fn, n_streams, qt, k, vt, gate, extra(tq),
                    q_tiles=n_lat // tq, q_tile0=0, tq=tq,
                    k_len=seq, k_blk0=0, tk=tk, name=name, **common)
    if not update_ctx:
        return y
    y_ctx = _flash_call(kernel_fn, n_streams, qt, k, vt, gate, extra(n_ctx),
                        q_tiles=1, q_tile0=n_lat // n_ctx, tq=n_ctx,
                        k_len=n_ctx, k_blk0=n_lat // n_ctx, tk=n_ctx,
                        name=name + "_ctx", **common)
    return jnp.concatenate([y, y_ctx], axis=1)


def _post_kernel(n_lat_tiles, n_y, *refs):
    y_refs = refs[:n_y]
    wo_ref, x_ref, mod_ref, o_ref = refs[n_y:]
    d = x_ref.shape[-1]
    is_ctx = pl.program_id(0) >= n_lat_tiles
    acc = None
    off = 0
    for y_ref in y_refs:
        w = y_ref.shape[0]
        part = _dot(wo_ref[:, off:off + w], y_ref[...])
        acc = part if acc is None else acc + part
        off += w
    gate = jnp.where(is_ctx, mod_ref[1:2, 2 * d:3 * d], mod_ref[0:1, 2 * d:3 * d])
    o_ref[...] = x_ref[...] + gate * jnp.transpose(acc)


def _post(ys, w_out, xcat, mod, n_lat, n_rows):
    d = xcat.shape[1]
    t = ROW_TILE
    width = w_out.shape[0]
    in_specs = [pl.BlockSpec((y.shape[0], t), lambda i: (0, i)) for y in ys] + [
        pl.BlockSpec((d, width), lambda i: (0, 0)),
        pl.BlockSpec((t, d), lambda i: (i, 0)),
        pl.BlockSpec((8, 3 * d), lambda i: (0, 0)),
    ]
    return pl.pallas_call(
        functools.partial(_post_kernel, n_lat // t, len(ys)),
        grid=(n_rows // t,),
        in_specs=in_specs,
        out_specs=pl.BlockSpec((t, d), lambda i: (i, 0)),
        out_shape=jax.ShapeDtypeStruct((n_rows, d), F32),
        compiler_params=_compiler_params(1),
        name="post",
    )(*ys, w_out.T.astype(BF16), xcat, mod)


def _rope_tables(n_lat, n_ctx, dim):
    rows, cols = jnp.meshgrid(jnp.arange(n_lat // GRID_W, dtype=F32),
                              jnp.arange(GRID_W, dtype=F32), indexing="ij")
    rows, cols = rows.reshape(-1), cols.reshape(-1)
    n_freq = dim // 4
    inv = ROPE_THETA ** (-jnp.arange(n_freq, dtype=F32) / n_freq)
    ang = jnp.concatenate([rows[:, None] * inv, cols[:, None] * inv], axis=-1)
    ang = jnp.concatenate([ang, jnp.zeros((n_ctx, dim // 2), F32)], axis=0)
    return jnp.transpose(jnp.cos(ang)), jnp.transpose(jnp.sin(ang))


def kernel(x, c, ctx, c_ctx, e_norm_g, e_w_mod, e_b_mod, e_w_in, a_g_cq, a_w_uq, a_g_ckv, a_w_ukv, a_g_qn, a_g_kn, b_g_qn, b_g_kn, b_lam_q1, b_lam_k1, b_lam_q2, b_lam_k2, b_g_out, e_w_out, o_norm_g, o_w_mod, o_b_mod, o_w_in, c_g_qn, c_g_kn, o_w_out):
    batch, n_lat, d = x.shape
    n_ctx = ctx.shape[1]
    assert batch == 1 and n_ctx == ROW_TILE and n_lat % ROW_TILE == 0
    depth = e_norm_g.shape[0] + o_norm_g.shape[0]

    cvec = jnp.zeros((8, d), F32).at[0].set(c[0]).at[1].set(c_ctx)
    rope_a = _rope_tables(n_lat, n_ctx, A_ROPE)
    rope_b = _rope_tables(n_lat, n_ctx, B_D)
    rope_c = _rope_tables(n_lat, n_ctx, C_HD)

    xcat = jnp.concatenate([x[0], ctx[0]], axis=0)
    for layer in range(depth):
        i = layer // 2
        update_ctx = layer < depth - 1
        n_rows = n_lat + n_ctx if update_ctx else n_lat
        pf = dict(n_lat=n_lat, n_ctx=n_ctx, update_ctx=update_ctx)
        if layer % 2 == 0:
            lam_init = 0.8 - 0.6 * math.exp(-0.3 * layer)
            mod = _modulation(cvec, e_w_mod[i], e_b_mod[i])
            qta, ka, vta, gta, qtb, kb, vtb, gtb = _even_pre(
                xcat, mod, n_lat, e_norm_g[i], e_w_in[i], a_g_cq[i], a_w_uq[i],
                a_g_ckv[i], a_w_ukv[i], a_g_qn[i], a_g_kn[i], b_g_qn[i], b_g_kn[i],
                rope_a, rope_b)
            ya = _prefix_flash(_flash_gate_kernel, 1, qta, ka, vta, gta, lambda tq: [],
                               n_heads=A_HEADS, kv_group=1, name="flash_mla", **pf)
            lam_vecs = jnp.stack([b_lam_q1[i], b_lam_k1[i], b_lam_q2[i], b_lam_k2[i]]).astype(F32)
            yb = _prefix_flash(
                functools.partial(_flash_diff_kernel, lam_init), 2, qtb, kb, vtb, gtb,
                lambda tq: [lam_vecs, _bcast_rows(b_g_out[i], tq)],
                n_heads=B_HEADS, kv_group=1, name="flash_diff", **pf)
            xcat = _post([ya, yb], e_w_out[i], xcat, mod, n_lat, n_rows)
        else:
            mod = _modulation(cvec, o_w_mod[i], o_b_mod[i])
            qtc, kc, vtc, gtc = _odd_pre(xcat, mod, n_lat, o_norm_g[i], o_w_in[i],
                                         c_g_qn[i], c_g_kn[i], rope_c)
            yc = _prefix_flash(_flash_gate_kernel, 1, qtc, kc, vtc, gtc, lambda tq: [],
                               n_heads=C_HEADS, kv_group=C_HEADS // C_KV_HEADS,
                               name="flash_gqa", **pf)
            xcat = _post([yc], o_w_out[i], xcat, mod, n_lat, n_rows)
    return xcat[:n_lat].reshape(batch, n_lat, d)
```

```python
import functools
import math

import jax
import jax.numpy as jnp
from jax import lax
from jax.experimental import pallas as pl
from jax.experimental.pallas import tpu as pltpu

F32 = jnp.float32
BF16 = jnp.bfloat16

GRID_W = 64
ROPE_THETA = 10000.0
EPS = 1e-6
LOG2E = math.log2(math.e)

A_HEADS, A_Q_LORA, A_KV_LORA, A_NOPE, A_ROPE, A_V = 8, 256, 128, 64, 32, 64
A_QK = A_NOPE + A_ROPE
B_HEADS, B_D = 4, 64
B_V = 2 * B_D
C_HEADS, C_KV_HEADS, C_HD = 8, 2, 128

HEAD_PAD = 128
ROW_TILE = 256
CHUNKS_PER_TRIP = 4
NEG_BIG = -1e30
SCORE_BOUND_LIMIT = 48.0
BOUND_MARGIN = 1.0 + 2.0 ** -8
VMEM_LIMIT_BYTES = 56 * 1024 * 1024


def _silu(v):
    return v / (1.0 + jnp.exp(-v))


def _dot(a, b):
    return jnp.dot(a, b, preferred_element_type=F32)


def _dot_nt(a, b):
    return lax.dot_general(a, b, (((1,), (1,)), ((), ())), preferred_element_type=F32)


def _rms_rows(v, g):
    ms = jnp.mean(v * v, axis=0, keepdims=True)
    return v * lax.rsqrt(ms + EPS) * g


def _round_with_sqnorm(v):
    vb = v.astype(BF16)
    vf = vb.astype(F32)
    return vb, jnp.sum(vf * vf, axis=0, keepdims=True)


def _rope_rows(v, cos, sin):
    half = v.shape[0] // 2
    x1, x2 = v[:half], v[half:]
    return jnp.concatenate([x1 * cos - x2 * sin, x2 * cos + x1 * sin], axis=0)


def _compiler_params(n_grid):
    return pltpu.CompilerParams(
        dimension_semantics=("arbitrary",) * n_grid,
        vmem_limit_bytes=VMEM_LIMIT_BYTES,
    )


def _mod_kernel(c_ref, w_ref, b_ref, o_ref):
    a = _silu(c_ref[...])
    w = w_ref[...]
    a_hi = a.astype(BF16)
    a_lo = (a - a_hi.astype(F32)).astype(BF16)
    w_hi = w.astype(BF16)
    w_lo = (w - w_hi.astype(F32)).astype(BF16)
    acc = _dot(a_hi, w_hi) + _dot(a_hi, w_lo) + _dot(a_lo, w_hi)
    o_ref[...] = acc + b_ref[...]


def _modulation(cvec, w_mod, b_mod):
    d, d3 = w_mod.shape
    col = 768
    return pl.pallas_call(
        _mod_kernel,
        grid=(d3 // col,),
        in_specs=[
            pl.BlockSpec((8, d), lambda j: (0, 0)),
            pl.BlockSpec((d, col), lambda j: (0, j)),
            pl.BlockSpec((1, col), lambda j: (0, j)),
        ],
        out_specs=pl.BlockSpec((8, col), lambda j: (0, j)),
        out_shape=jax.ShapeDtypeStruct((8, d3), F32),
        compiler_params=_compiler_params(1),
        name="modulation",
    )(cvec, w_mod, b_mod.reshape(1, d3))


def _modulated_rows(x_ref, mod_ref, g_ref, is_ctx):
    d = x_ref.shape[-1]
    x = x_ref[...]
    sh = jnp.where(is_ctx, mod_ref[1:2, 0:d], mod_ref[0:1, 0:d])
    sc = jnp.where(is_ctx, mod_ref[1:2, d:2 * d], mod_ref[0:1, d:2 * d])
    ms = jnp.mean(x * x, axis=-1, keepdims=True)
    y = x * lax.rsqrt(ms + EPS) * g_ref[...]
    return (y * (1.0 + sc) + sh).astype(BF16)


def _even_pre_kernel(n_lat_tiles,
                     x_ref, mod_ref, g_ref, wt_ref,
                     gcq_ref, wuq_ref, gckv_ref, wukv_ref, gqa_ref, gka_ref,
                     cosa_ref, sina_ref, gqb_ref, gkb_ref, cosb_ref, sinb_ref,
                     qta_ref, ka_ref, vta_ref, gta_ref, qna_ref, kna_ref,
                     qtb_ref, kb_ref, vtb_ref, gtb_ref, qnb_ref, knb_ref):
    is_ctx = pl.program_id(0) >= n_lat_tiles
    hb = _modulated_rows(x_ref, mod_ref, g_ref, is_ctx)
    tok = hb.shape[0]

    o_cq, o_ckv, o_kr = 0, A_Q_LORA, A_Q_LORA + A_KV_LORA
    o_ga = o_kr + A_ROPE
    o_bq = o_ga + A_HEADS * A_V
    o_bk = o_bq + 2 * B_HEADS * B_D
    o_bv = o_bk + 2 * B_HEADS * B_D
    o_gb = o_bv + B_HEADS * B_V
    o_end = o_gb + B_HEADS * B_V

    def proj(lo, hi):
        return _dot_nt(wt_ref[lo:hi, :], hb)

    cq = _rms_rows(proj(o_cq, o_ckv), gcq_ref[...]).astype(BF16)
    q_all = _dot(wuq_ref[...], cq)
    cos_a, sin_a = cosa_ref[...], sina_ref[...]
    q_scale = (A_QK ** -0.5) * LOG2E
    pad_a = jnp.zeros((HEAD_PAD - A_QK, tok), F32)
    norms = []
    for h in range(A_HEADS):
        qh = _rms_rows(q_all[h * A_QK:(h + 1) * A_QK], gqa_ref[...])
        qh = jnp.concatenate(
            [qh[:A_NOPE], _rope_rows(qh[A_NOPE:], cos_a, sin_a), pad_a], axis=0)
        qta_ref[h], n2 = _round_with_sqnorm(qh * q_scale)
        norms.append(n2)
    qna_ref[...] = jnp.concatenate(norms, axis=0)

    ckv_kr = proj(o_ckv, o_ga)
    ckv = _rms_rows(ckv_kr[:A_KV_LORA], gckv_ref[...]).astype(BF16)
    kr = ckv_kr[A_KV_LORA:]
    kv_all = _dot(wukv_ref[...], ckv)
    norms = []
    for h in range(A_HEADS):
        base = h * (A_NOPE + A_V)
        kh = jnp.concatenate([kv_all[base:base + A_NOPE], kr], axis=0)
        kh = _rms_rows(kh, gka_ref[...])
        kh = jnp.concatenate(
            [kh[:A_NOPE], _rope_rows(kh[A_NOPE:], cos_a, sin_a), pad_a], axis=0)
        norms.append(_round_with_sqnorm(kh)[1])
        ka_ref[h] = jnp.transpose(kh).astype(BF16)
        vta_ref[h] = kv_all[base + A_NOPE:base + A_NOPE + A_V].astype(BF16)
    kna_ref[...] = jnp.concatenate(norms, axis=0)

    gta_ref[...] = _silu(proj(o_ga, o_bq))

    cos_b, sin_b = cosb_ref[...], sinb_ref[...]
    qb_scale = (B_D ** -0.5) * LOG2E
    pad_b = jnp.zeros((B_D, tok), F32)
    bq = proj(o_bq, o_bk)
    bk = proj(o_bk, o_bv)
    q_norms, k_norms = [], []
    for h in range(B_HEADS):
        ks = []
        for j in range(2):
            m = 2 * h + j
            qm = _rms_rows(bq[m * B_D:(m + 1) * B_D], gqb_ref[...])
            qm = _rope_rows(qm, cos_b, sin_b) * qb_scale
            parts = [qm, pad_b] if j == 0 else [pad_b, qm]
            qtb_ref[m], n2 = _round_with_sqnorm(jnp.concatenate(parts, axis=0))
            q_norms.append(n2)
            km = _rms_rows(bk[m * B_D:(m + 1) * B_D], gkb_ref[...])
            km = _rope_rows(km, cos_b, sin_b)
            k_norms.append(_round_with_sqnorm(km)[1])
            ks.append(km)
        kb_ref[h] = jnp.transpose(jnp.concatenate(ks, axis=0)).astype(BF16)
    qnb_ref[...] = jnp.concatenate(q_norms, axis=0)
    knb_ref[...] = jnp.concatenate(k_norms, axis=0)
    bv = proj(o_bv, o_gb)
    for h in range(B_HEADS):
        vtb_ref[h] = bv[h * B_V:(h + 1) * B_V].astype(BF16)
    gtb_ref[...] = _silu(proj(o_gb, o_end))


def _bcast_rows(g, tok):
    return jnp.broadcast_to(g.astype(F32)[:, None], (g.shape[0], tok))


def _even_pre(xcat, mod, n_lat, norm_g, w_in, a_g_cq, a_w_uq, a_g_ckv, a_w_ukv,
              a_g_qn, a_g_kn, b_g_qn, b_g_kn, rope_a, rope_b):
    seq, d = xcat.shape
    t = ROW_TILE
    n_tiles = seq // t
    e_in = w_in.shape[1]
    const2 = lambda i: (0, 0)
    tok2 = lambda i: (0, i)
    tok3 = lambda i: (0, 0, i)
    row3 = lambda i: (0, i, 0)
    in_specs = [
        pl.BlockSpec((t, d), lambda i: (i, 0)),
        pl.BlockSpec((8, 3 * d), const2),
        pl.BlockSpec((1, d), const2),
        pl.BlockSpec((e_in, d), const2),
        pl.BlockSpec((A_Q_LORA, t), const2),
        pl.BlockSpec((A_HEADS * A_QK, A_Q_LORA), const2),
        pl.BlockSpec((A_KV_LORA, t), const2),
        pl.BlockSpec((A_HEADS * (A_NOPE + A_V), A_KV_LORA), const2),
        pl.BlockSpec((A_QK, t), const2),
        pl.BlockSpec((A_QK, t), const2),
        pl.BlockSpec((A_ROPE // 2, t), tok2),
        pl.BlockSpec((A_ROPE // 2, t), tok2),
        pl.BlockSpec((B_D, t), const2),
        pl.BlockSpec((B_D, t), const2),
        pl.BlockSpec((B_D // 2, t), tok2),
        pl.BlockSpec((B_D // 2, t), tok2),
    ]
    out_shape = [
        jax.ShapeDtypeStruct((A_HEADS, HEAD_PAD, seq), BF16),
        jax.ShapeDtypeStruct((A_HEADS, seq, HEAD_PAD), BF16),
        jax.ShapeDtypeStruct((A_HEADS, A_V, seq), BF16),
        jax.ShapeDtypeStruct((A_HEADS * A_V, seq), F32),
        jax.ShapeDtypeStruct((A_HEADS, seq), F32),
        jax.ShapeDtypeStruct((A_HEADS, seq), F32),
        jax.ShapeDtypeStruct((2 * B_HEADS, HEAD_PAD, seq), BF16),
        jax.ShapeDtypeStruct((B_HEADS, seq, HEAD_PAD), BF16),
        jax.ShapeDtypeStruct((B_HEADS, B_V, seq), BF16),
        jax.ShapeDtypeStruct((B_HEADS * B_V, seq), F32),
        jax.ShapeDtypeStruct((2 * B_HEADS, seq), F32),
        jax.ShapeDtypeStruct((2 * B_HEADS, seq), F32),
    ]
    out_specs = [
        pl.BlockSpec((A_HEADS, HEAD_PAD, t), tok3),
        pl.BlockSpec((A_HEADS, t, HEAD_PAD), row3),
        pl.BlockSpec((A_HEADS, A_V, t), tok3),
        pl.BlockSpec((A_HEADS * A_V, t), tok2),
        pl.BlockSpec((A_HEADS, t), tok2),
        pl.BlockSpec((A_HEADS, t), tok2),
        pl.BlockSpec((2 * B_HEADS, HEAD_PAD, t), tok3),
        pl.BlockSpec((B_HEADS, t, HEAD_PAD), row3),
        pl.BlockSpec((B_HEADS, B_V, t), tok3),
        pl.BlockSpec((B_HEADS * B_V, t), tok2),
        pl.BlockSpec((2 * B_HEADS, t), tok2),
        pl.BlockSpec((2 * B_HEADS, t), tok2),
    ]
    return pl.pallas_call(
        functools.partial(_even_pre_kernel, n_lat // t),
        grid=(n_tiles,),
        in_specs=in_specs,
        out_specs=out_specs,
        out_shape=out_shape,
        compiler_params=_compiler_params(1),
        name="even_pre",
    )(xcat, mod, norm_g.reshape(1, d), w_in.T.astype(BF16),
      _bcast_rows(a_g_cq, t), a_w_uq.T.astype(BF16),
      _bcast_rows(a_g_ckv, t), a_w_ukv.T.astype(BF16),
      _bcast_rows(a_g_qn, t), _bcast_rows(a_g_kn, t), rope_a[0], rope_a[1],
      _bcast_rows(b_g_qn, t), _bcast_rows(b_g_kn, t), rope_b[0], rope_b[1])


def _odd_pre_kernel(n_lat_tiles,
                    x_ref, mod_ref, g_ref, wt_ref, gq_ref, gk_ref, cos_ref, sin_ref,
                    qt_ref, k_ref, vt_ref, gt_ref, qn_ref, kn_ref):
    is_ctx = pl.program_id(0) >= n_lat_tiles
    hb = _modulated_rows(x_ref, mod_ref, g_ref, is_ctx)
    cos, sin = cos_ref[...], sin_ref[...]
    o_k = C_HEADS * C_HD
    o_v = o_k + C_KV_HEADS * C_HD
    o_g = o_v + C_KV_HEADS * C_HD
    o_end = o_g + C_HEADS * C_HD
    q_scale = (C_HD ** -0.5) * LOG2E
    q = _dot_nt(wt_ref[0:o_k, :], hb)
    norms = []
    for h in range(C_HEADS):
        qh = _rms_rows(q[h * C_HD:(h + 1) * C_HD], gq_ref[...])
        qt_ref[h], n2 = _round_with_sqnorm(_rope_rows(qh, cos, sin) * q_scale)
        norms.append(n2)
    qn_ref[...] = jnp.concatenate(norms, axis=0)
    kv = _dot_nt(wt_ref[o_k:o_g, :], hb)
    norms = []
    for h in range(C_KV_HEADS):
        kh = _rms_rows(kv[h * C_HD:(h + 1) * C_HD], gk_ref[...])
        kh = _rope_rows(kh, cos, sin)
        norms.append(_round_with_sqnorm(kh)[1])
        k_ref[h] = jnp.transpose(kh).astype(BF16)
        vt_ref[h] = kv[(C_KV_HEADS + h) * C_HD:(C_KV_HEADS + h + 1) * C_HD].astype(BF16)
    kn_ref[...] = jnp.concatenate(norms, axis=0)
    gt_ref[...] = _silu(_dot_nt(wt_ref[o_g:o_end, :], hb))


def _odd_pre(xcat, mod, n_lat, norm_g, w_in, g_qn, g_kn, rope_c):
    seq, d = xcat.shape
    t = ROW_TILE
    o_in = w_in.shape[1]
    const2 = lambda i: (0, 0)
    tok2 = lambda i: (0, i)
    tok3 = lambda i: (0, 0, i)
    in_specs = [
        pl.BlockSpec((t, d), lambda i: (i, 0)),
        pl.BlockSpec((8, 3 * d), const2),
        pl.BlockSpec((1, d), const2),
        pl.BlockSpec((o_in, d), const2),
        pl.BlockSpec((C_HD, t), const2),
        pl.BlockSpec((C_HD, t), const2),
        pl.BlockSpec((C_HD // 2, t), tok2),
        pl.BlockSpec((C_HD // 2, t), tok2),
    ]
    out_shape = [
        jax.ShapeDtypeStruct((C_HEADS, C_HD, seq), BF16),
        jax.ShapeDtypeStruct((C_KV_HEADS, seq, C_HD), BF16),
        jax.ShapeDtypeStruct((C_KV_HEADS, C_HD, seq), BF16),
        jax.ShapeDtypeStruct((C_HEADS * C_HD, seq), F32),
        jax.ShapeDtypeStruct((C_HEADS, seq), F32),
        jax.ShapeDtypeStruct((C_KV_HEADS, seq), F32),
    ]
    out_specs = [
        pl.BlockSpec((C_HEADS, C_HD, t), tok3),
        pl.BlockSpec((C_KV_HEADS, t, C_HD), lambda i: (0, i, 0)),
        pl.BlockSpec((C_KV_HEADS, C_HD, t), tok3),
        pl.BlockSpec((C_HEADS * C_HD, t), tok2),
        pl.BlockSpec((C_HEADS, t), tok2),
        pl.BlockSpec((C_KV_HEADS, t), tok2),
    ]
    return pl.pallas_call(
        functools.partial(_odd_pre_kernel, n_lat // t),
        grid=(seq // t,),
        in_specs=in_specs,
        out_specs=out_specs,
        out_shape=out_shape,
        compiler_params=_compiler_params(1),
        name="odd_pre",
    )(xcat, mod, norm_g.reshape(1, d), w_in.T.astype(BF16),
      _bcast_rows(g_qn, t), _bcast_rows(g_kn, t), rope_c[0], rope_c[1])


def _flash_streams(qts, k_ref, vt_ref, s_ref, acc_ref, n_chunks, tk):
    n_s = len(qts)
    tq = qts[0].shape[1]
    for st in range(n_s):
        acc_ref[st] = jnp.zeros(acc_ref.shape[1:], F32)

    def scores(slot, c):
        start = pl.multiple_of(c * tk, tk)
        k = k_ref[0, pl.ds(start, tk), :]
        maxes = []
        for st in range(n_s):
            s = _dot(k, qts[st])
            s_ref[st, slot] = s
            maxes.append(jnp.max(s, axis=0, keepdims=True))
        return tuple(maxes)

    def consume(slot, c, state, maxes):
        ms, ls = state
        start = pl.multiple_of(c * tk, tk)
        vt = vt_ref[0, :, pl.ds(start, tk)]
        new_ms, new_ls = [], []
        for st in range(n_s):
            m_new = jnp.maximum(ms[st], maxes[st])
            alpha = jnp.exp2(ms[st] - m_new)
            p = jnp.exp2(s_ref[st, slot] - m_new)
            new_ls.append(alpha * ls[st] + jnp.sum(p, axis=0, keepdims=True))
            acc_ref[st] = alpha * acc_ref[st] + _dot(vt, p.astype(BF16))
            new_ms.append(m_new)
        return tuple(new_ms), tuple(new_ls)

    def step(c, slot, state, mx_cur):
        mx_next = scores(1 - slot, c + 1)
        return consume(slot, c, state, mx_cur), mx_next

    def group(j, carry):
        state, mx = carry
        for u in range(CHUNKS_PER_TRIP):
            state, mx = step(j * CHUNKS_PER_TRIP + u, u % 2, state, mx)
        return state, mx

    state = (tuple(jnp.full((1, tq), NEG_BIG, F32) for _ in range(n_s)),
             tuple(jnp.zeros((1, tq), F32) for _ in range(n_s)))
    mx = scores(0, 0)
    n_trips = (n_chunks - 1) // CHUNKS_PER_TRIP
    state, mx = lax.fori_loop(0, n_trips, group, (state, mx))
    for c in range(n_trips * CHUNKS_PER_TRIP, n_chunks - 1):
        state, mx = step(c, c % 2, state, mx)
    state = consume((n_chunks - 1) % 2, n_chunks - 1, state, mx)
    _, ls = state
    return [acc_ref[st] / ls[st] for st in range(n_s)]


def _bounded_streams(qts, bounds, k_ref, vt_ref, acc_ref, n_chunks, tk):
    n_s = len(qts)
    ls = [None] * n_s
    for c in range(n_chunks):
        k = k_ref[0, c * tk:(c + 1) * tk, :]
        vt = vt_ref[0, :, c * tk:(c + 1) * tk]
        for st in range(n_s):
            p = jnp.exp2(_dot(k, qts[st]) - bounds[st])
            p_sum = jnp.sum(p, axis=0, keepdims=True)
            pv = _dot(vt, p.astype(BF16))
            if c == 0:
                ls[st] = p_sum
                acc_ref[st] = pv
            else:
                ls[st] = ls[st] + p_sum
                acc_ref[st] += pv
    return [acc_ref[st] / ls[st] for st in range(n_s)]


def _attend(bounded, n_s, n_chunks, tk, refs):
    qts = [r[0] for r in refs[:n_s]]
    pos = n_s
    if bounded:
        bounds = [r[0] for r in refs[pos:pos + n_s]]
        pos += n_s
    k_ref, vt_ref = refs[pos:pos + 2]
    pos += 2
    if bounded:
        yt_ref, acc_ref = refs[-2:]
        outs = _bounded_streams(qts, bounds, k_ref, vt_ref, acc_ref, n_chunks, tk)
        return outs, refs[pos:-2], yt_ref
    yt_ref, s_ref, acc_ref = refs[-3:]
    outs = _flash_streams(qts, k_ref, vt_ref, s_ref, acc_ref, n_chunks, tk)
    return outs, refs[pos:-3], yt_ref


def _flash_gate_kernel(bounded, n_chunks, tk, *refs):
    (o,), (gate_ref,), yt_ref = _attend(bounded, 1, n_chunks, tk, refs)
    yt_ref[...] = (o * gate_ref[...]).astype(BF16)


def _flash_diff_kernel(lam_init, bounded, n_chunks, tk, *refs):
    (o1, o2), (gate_ref, lam_ref, gout_ref), yt_ref = _attend(bounded, 2, n_chunks, tk, refs)
    lv = lam_ref[...]
    lam = (jnp.exp(jnp.sum(lv[0:1] * lv[1:2], axis=-1, keepdims=True))
           - jnp.exp(jnp.sum(lv[2:3] * lv[3:4], axis=-1, keepdims=True)) + lam_init)
    o = o1 - lam * o2
    o = _rms_rows(o, gout_ref[...]) * (1.0 - lam_init)
    yt_ref[...] = (o * gate_ref[...]).astype(BF16)


def _flash_call(kernel_fn, n_streams, qt, bound, k, vt, gate, extra, *, n_heads, kv_group,
                q_tiles, q_tile0, tq, k_len, k_blk0, tk, name):
    dv = vt.shape[1]
    n_chunks = k_len // tk
    bounded = bound is not None

    def stream_specs(rows):
        return [
            pl.BlockSpec((1, rows, tq),
                         functools.partial(lambda s, h, i: (n_streams * h + s, 0, q_tile0 + i), s))
            for s in range(n_streams)
        ]

    in_specs = stream_specs(HEAD_PAD) + (stream_specs(1) if bounded else []) + [
        pl.BlockSpec((1, k_len, HEAD_PAD), lambda h, i: (h // kv_group, k_blk0, 0)),
        pl.BlockSpec((1, dv, k_len), lambda h, i: (h // kv_group, 0, k_blk0)),
        pl.BlockSpec((dv, tq), lambda h, i: (h, q_tile0 + i)),
    ] + [pl.BlockSpec(e.shape, lambda h, i: (0, 0)) for e in extra]
    scratch = [pltpu.VMEM((n_streams, dv, tq), F32)]
    if not bounded:
        scratch = [pltpu.VMEM((n_streams, 2, tk, tq), F32)] + scratch
    args = [qt] * n_streams + ([bound] * n_streams if bounded else [])
    return pl.pallas_call(
        functools.partial(kernel_fn, bounded, n_chunks, tk),
        grid=(n_heads, q_tiles),
        in_specs=in_specs,
        out_specs=pl.BlockSpec((dv, tq), lambda h, i: (h, i)),
        out_shape=jax.ShapeDtypeStruct((n_heads * dv, q_tiles * tq), BF16),
        scratch_shapes=scratch,
        compiler_params=_compiler_params(2),
        name=name,
    )(*args, k, vt, gate, *extra)


def _prefix_flash(kernel_fn, n_streams, qt, k, vt, gate, qn2, kn2, extra, *, n_heads,
                  kv_group, n_lat, n_ctx, update_ctx, name):
    seq = n_lat + n_ctx
    tq = 512 if n_lat % 512 == 0 else ROW_TILE
    tk = 1280 if seq % 1280 == 0 else ROW_TILE
    common = dict(n_heads=n_heads, kv_group=kv_group)
    lat = dict(q_tiles=n_lat // tq, q_tile0=0, tq=tq, k_len=seq, k_blk0=0, tk=tk, **common)

    k_max2 = jnp.repeat(jnp.max(kn2, axis=1), qn2.shape[0] // kn2.shape[0])
    bound = jnp.sqrt(qn2[:, :n_lat] * k_max2[:, None]) * BOUND_MARGIN
    ex = extra(tq)

    def bounded(*a):
        return _flash_call(kernel_fn, n_streams, a[0], a[1], *a[2:5], a[5:], name=name, **lat)

    def online(*a):
        return _flash_call(kernel_fn, n_streams, a[0], None, *a[2:5], a[5:],
                           name=name + "_online", **lat)

    y = lax.cond(jnp.max(bound) <= SCORE_BOUND_LIMIT, bounded, online,
                 qt, bound[:, None, :], k, vt, gate, *ex)
    if not update_ctx:
        return y
    y_ctx = _flash_call(kernel_fn, n_streams, qt, None, k, vt, gate, extra(n_ctx),
                        q_tiles=1, q_tile0=n_lat // n_ctx, tq=n_ctx,
                        k_len=n_ctx, k_blk0=n_lat // n_ctx, tk=n_ctx,
                        name=name + "_ctx", **common)
    return jnp.concatenate([y, y_ctx], axis=1)


def _post_kernel(n_lat_tiles, n_y, *refs):
    y_refs = refs[:n_y]
    wo_ref, x_ref, mod_ref, o_ref = refs[n_y:]
    d = x_ref.shape[-1]
    is_ctx = pl.program_id(0) >= n_lat_tiles
    acc = None
    off = 0
    for y_ref in y_refs:
        w = y_ref.shape[0]
        part = _dot(wo_ref[:, off:off + w], y_ref[...])
        acc = part if acc is None else acc + part
        off += w
    gate = jnp.where(is_ctx, mod_ref[1:2, 2 * d:3 * d], mod_ref[0:1, 2 * d:3 * d])
    o_ref[...] = x_ref[...] + gate * jnp.transpose(acc)


def _post(ys, w_out, xcat, mod, n_lat, n_rows):
    d = xcat.shape[1]
    t = ROW_TILE
    width = w_out.shape[0]
    in_specs = [pl.BlockSpec((y.shape[0], t), lambda i: (0, i)) for y in ys] + [
        pl.BlockSpec((d, width), lambda i: (0, 0)),
        pl.BlockSpec((t, d), lambda i: (i, 0)),
        pl.BlockSpec((8, 3 * d), lambda i: (0, 0)),
    ]
    return pl.pallas_call(
        functools.partial(_post_kernel, n_lat // t, len(ys)),
        grid=(n_rows // t,),
        in_specs=in_specs,
        out_specs=pl.BlockSpec((t, d), lambda i: (i, 0)),
        out_shape=jax.ShapeDtypeStruct((n_rows, d), F32),
        compiler_params=_compiler_params(1),
        name="post",
    )(*ys, w_out.T.astype(BF16), xcat, mod)


def _rope_tables(n_lat, n_ctx, dim):
    rows, cols = jnp.meshgrid(jnp.arange(n_lat // GRID_W, dtype=F32),
                              jnp.arange(GRID_W, dtype=F32), indexing="ij")
    rows, cols = rows.reshape(-1), cols.reshape(-1)
    n_freq = dim // 4
    inv = ROPE_THETA ** (-jnp.arange(n_freq, dtype=F32) / n_freq)
    ang = jnp.concatenate([rows[:, None] * inv, cols[:, None] * inv], axis=-1)
    ang = jnp.concatenate([ang, jnp.zeros((n_ctx, dim // 2), F32)], axis=0)
    return jnp.transpose(jnp.cos(ang)), jnp.transpose(jnp.sin(ang))


def kernel(x, c, ctx, c_ctx, e_norm_g, e_w_mod, e_b_mod, e_w_in, a_g_cq, a_w_uq, a_g_ckv, a_w_ukv, a_g_qn, a_g_kn, b_g_qn, b_g_kn, b_lam_q1, b_lam_k1, b_lam_q2, b_lam_k2, b_g_out, e_w_out, o_norm_g, o_w_mod, o_b_mod, o_w_in, c_g_qn, c_g_kn, o_w_out):
    batch, n_lat, d = x.shape
    n_ctx = ctx.shape[1]
    assert batch == 1 and n_ctx == ROW_TILE and n_lat % ROW_TILE == 0
    depth = e_norm_g.shape[0] + o_norm_g.shape[0]

    cvec = jnp.zeros((8, d), F32).at[0].set(c[0]).at[1].set(c_ctx)
    rope_a = _rope_tables(n_lat, n_ctx, A_ROPE)
    rope_b = _rope_tables(n_lat, n_ctx, B_D)
    rope_c = _rope_tables(n_lat, n_ctx, C_HD)

    xcat = jnp.concatenate([x[0], ctx[0]], axis=0)
    for layer in range(depth):
        i = layer // 2
        update_ctx = layer < depth - 1
        n_rows = n_lat + n_ctx if update_ctx else n_lat
        pf = dict(n_lat=n_lat, n_ctx=n_ctx, update_ctx=update_ctx)
        if layer % 2 == 0:
            lam_init = 0.8 - 0.6 * math.exp(-0.3 * layer)
            mod = _modulation(cvec, e_w_mod[i], e_b_mod[i])
            qta, ka, vta, gta, qna, kna, qtb, kb, vtb, gtb, qnb, knb = _even_pre(
                xcat, mod, n_lat, e_norm_g[i], e_w_in[i], a_g_cq[i], a_w_uq[i],
                a_g_ckv[i], a_w_ukv[i], a_g_qn[i], a_g_kn[i], b_g_qn[i], b_g_kn[i],
                rope_a, rope_b)
            ya = _prefix_flash(_flash_gate_kernel, 1, qta, ka, vta, gta, qna, kna,
                               lambda tq: [], n_heads=A_HEADS, kv_group=1,
                               name="flash_mla", **pf)
            lam_vecs = jnp.stack([b_lam_q1[i], b_lam_k1[i], b_lam_q2[i], b_lam_k2[i]]).astype(F32)
            yb = _prefix_flash(
                functools.partial(_flash_diff_kernel, lam_init), 2, qtb, kb, vtb, gtb, qnb, knb,
                lambda tq: [lam_vecs, _bcast_rows(b_g_out[i], tq)],
                n_heads=B_HEADS, kv_group=1, name="flash_diff", **pf)
            xcat = _post([ya, yb], e_w_out[i], xcat, mod, n_lat, n_rows)
        else:
            mod = _modulation(cvec, o_w_mod[i], o_b_mod[i])
            qtc, kc, vtc, gtc, qnc, knc = _odd_pre(
                xcat, mod, n_lat, o_norm_g[i], o_w_in[i], c_g_qn[i], c_g_kn[i], rope_c)
            yc = _prefix_flash(_flash_gate_kernel, 1, qtc, kc, vtc, gtc, qnc, knc,
                               lambda tq: [], n_heads=C_HEADS,
                               kv_group=C_HEADS // C_KV_HEADS, name="flash_gqa", **pf)
            xcat = _post([yc], o_w_out[i], xcat, mod, n_lat, n_rows)
    return xcat[:n_lat].reshape(batch, n_lat, d)
```

```python
import functools
import math

import jax
import jax.numpy as jnp
import numpy as np
from jax import lax
from jax.experimental import pallas as pl
from jax.experimental.pallas import tpu as pltpu

F32 = jnp.float32
BF16 = jnp.bfloat16

GRID_W = 64
ROPE_THETA = 10000.0
EPS = 1e-6
LOG2E = math.log2(math.e)

A_HEADS, A_Q_LORA, A_KV_LORA, A_NOPE, A_ROPE, A_V = 8, 256, 128, 64, 32, 64
A_QK = A_NOPE + A_ROPE
B_HEADS, B_D = 4, 64
B_V = 2 * B_D
C_HEADS, C_KV_HEADS, C_HD = 8, 2, 128

HEAD_PAD = 128
ROW_TILE = 256
POST_TILE = 512
CHUNKS_PER_TRIP = 4
NEG_BIG = -1e30
SCORE_BOUND_LIMIT = 48.0
BOUND_MARGIN = 1.0 + 2.0 ** -8
VMEM_LIMIT_BYTES = 56 * 1024 * 1024


def _silu(v):
    return v / (1.0 + jnp.exp(-v))


def _dot(a, b):
    return jnp.dot(a, b, preferred_element_type=F32)


def _dot_nt(a, b):
    return lax.dot_general(a, b, (((1,), (1,)), ((), ())), preferred_element_type=F32)


def _rms_rows(v, g):
    ms = jnp.mean(v * v, axis=0, keepdims=True)
    return v * lax.rsqrt(ms + EPS) * g


def _round_with_sqnorm(v):
    vb = v.astype(BF16)
    vf = vb.astype(F32)
    return vb, jnp.sum(vf * vf, axis=0, keepdims=True)


def _rope_rows(v, cos, sin):
    half = v.shape[0] // 2
    x1, x2 = v[:half], v[half:]
    return jnp.concatenate([x1 * cos - x2 * sin, x2 * cos + x1 * sin], axis=0)


def _compiler_params(n_grid):
    return pltpu.CompilerParams(
        dimension_semantics=("arbitrary",) * n_grid,
        vmem_limit_bytes=VMEM_LIMIT_BYTES,
    )


def _mod_kernel(c_ref, w_ref, b_ref, o_ref):
    a = _silu(c_ref[...])
    w = w_ref[...]
    a_hi = a.astype(BF16)
    a_lo = (a - a_hi.astype(F32)).astype(BF16)
    w_hi = w.astype(BF16)
    w_lo = (w - w_hi.astype(F32)).astype(BF16)
    acc = _dot(a_hi, w_hi) + _dot(a_hi, w_lo) + _dot(a_lo, w_hi)
    o_ref[...] = acc + b_ref[...]


def _modulation(cvec, w_mod, b_mod):
    d, d3 = w_mod.shape
    col = 768
    return pl.pallas_call(
        _mod_kernel,
        grid=(d3 // col,),
        in_specs=[
            pl.BlockSpec((8, d), lambda j: (0, 0)),
            pl.BlockSpec((d, col), lambda j: (0, j)),
            pl.BlockSpec((1, col), lambda j: (0, j)),
        ],
        out_specs=pl.BlockSpec((8, col), lambda j: (0, j)),
        out_shape=jax.ShapeDtypeStruct((8, d3), F32),
        compiler_params=_compiler_params(1),
        name="modulation",
    )(cvec, w_mod, b_mod.reshape(1, d3))


def _token_specs(t, d, n_lat_tiles):
    return [pl.BlockSpec((t, d), lambda i: (jnp.minimum(i, n_lat_tiles - 1), 0)),
            pl.BlockSpec((t, d), lambda i: (0, 0))]


def _modulated_rows(x_ref, xc_ref, mod_ref, g_ref, is_ctx):
    d = x_ref.shape[-1]
    x = jnp.where(is_ctx, xc_ref[...], x_ref[...])
    sh = jnp.where(is_ctx, mod_ref[1:2, 0:d], mod_ref[0:1, 0:d])
    sc = jnp.where(is_ctx, mod_ref[1:2, d:2 * d], mod_ref[0:1, d:2 * d])
    ms = jnp.mean(x * x, axis=-1, keepdims=True)
    y = x * lax.rsqrt(ms + EPS) * g_ref[...]
    return (y * (1.0 + sc) + sh).astype(BF16)


def _even_pre_kernel(n_lat_tiles,
                     x_ref, xc_ref, mod_ref, g_ref, wt_ref,
                     gcq_ref, wuq_ref, gckv_ref, wukv_ref, gqa_ref, gka_ref,
                     cosa_ref, sina_ref, gqb_ref, gkb_ref, cosb_ref, sinb_ref,
                     qta_ref, ka_ref, vta_ref, gta_ref, qna_ref, kna_ref,
                     qtb_ref, kb_ref, vtb_ref, gtb_ref, qnb_ref, knb_ref):
    is_ctx = pl.program_id(0) >= n_lat_tiles
    hb = _modulated_rows(x_ref, xc_ref, mod_ref, g_ref, is_ctx)
    tok = hb.shape[0]

    o_cq, o_ckv, o_kr = 0, A_Q_LORA, A_Q_LORA + A_KV_LORA
    o_ga = o_kr + A_ROPE
    o_bq = o_ga + A_HEADS * A_V
    o_bk = o_bq + 2 * B_HEADS * B_D
    o_bv = o_bk + 2 * B_HEADS * B_D
    o_gb = o_bv + B_HEADS * B_V
    o_end = o_gb + B_HEADS * B_V

    def proj(lo, hi):
        return _dot_nt(wt_ref[lo:hi, :], hb)

    cq = _rms_rows(proj(o_cq, o_ckv), gcq_ref[...]).astype(BF16)
    q_all = _dot(wuq_ref[...], cq)
    cos_a, sin_a = cosa_ref[...], sina_ref[...]
    q_scale = (A_QK ** -0.5) * LOG2E
    pad_a = jnp.zeros((HEAD_PAD - A_QK, tok), F32)
    norms = []
    for h in range(A_HEADS):
        qh = _rms_rows(q_all[h * A_QK:(h + 1) * A_QK], gqa_ref[...])
        qh = jnp.concatenate(
            [qh[:A_NOPE], _rope_rows(qh[A_NOPE:], cos_a, sin_a), pad_a], axis=0)
        qta_ref[h], n2 = _round_with_sqnorm(qh * q_scale)
        norms.append(n2)
    qna_ref[...] = jnp.concatenate(norms, axis=0)

    ckv_kr = proj(o_ckv, o_ga)
    ckv = _rms_rows(ckv_kr[:A_KV_LORA], gckv_ref[...]).astype(BF16)
    kr = ckv_kr[A_KV_LORA:]
    kv_all = _dot(wukv_ref[...], ckv)
    norms = []
    for h in range(A_HEADS):
        base = h * (A_NOPE + A_V)
        kh = jnp.concatenate([kv_all[base:base + A_NOPE], kr], axis=0)
        kh = _rms_rows(kh, gka_ref[...])
        kh = jnp.concatenate(
            [kh[:A_NOPE], _rope_rows(kh[A_NOPE:], cos_a, sin_a), pad_a], axis=0)
        norms.append(_round_with_sqnorm(kh)[1])
        ka_ref[h] = jnp.transpose(kh).astype(BF16)
        vh = kv_all[base + A_NOPE:base + A_NOPE + A_V]
        vta_ref[h] = jnp.concatenate([vh, jnp.zeros_like(vh)], axis=0).astype(BF16)
    kna_ref[...] = jnp.concatenate(norms, axis=0)

    gta_ref[...] = _silu(proj(o_ga, o_bq))

    cos_b, sin_b = cosb_ref[...], sinb_ref[...]
    qb_scale = (B_D ** -0.5) * LOG2E
    pad_b = jnp.zeros((B_D, tok), F32)
    bq = proj(o_bq, o_bk)
    bk = proj(o_bk, o_bv)
    q_norms, k_norms = [], []
    for h in range(B_HEADS):
        ks = []
        for j in range(2):
            m = 2 * h + j
            qm = _rms_rows(bq[m * B_D:(m + 1) * B_D], gqb_ref[...])
            qm = _rope_rows(qm, cos_b, sin_b) * qb_scale
            parts = [qm, pad_b] if j == 0 else [pad_b, qm]
            qtb_ref[m], n2 = _round_with_sqnorm(jnp.concatenate(parts, axis=0))
            q_norms.append(n2)
            km = _rms_rows(bk[m * B_D:(m + 1) * B_D], gkb_ref[...])
            km = _rope_rows(km, cos_b, sin_b)
            k_norms.append(_round_with_sqnorm(km)[1])
            ks.append(km)
        kb_ref[h] = jnp.transpose(jnp.concatenate(ks, axis=0)).astype(BF16)
    qnb_ref[...] = jnp.concatenate(q_norms, axis=0)
    knb_ref[...] = jnp.concatenate(k_norms, axis=0)
    bv = proj(o_bv, o_gb)
    for h in range(B_HEADS):
        vtb_ref[h] = bv[h * B_V:(h + 1) * B_V].astype(BF16)
    gtb_ref[...] = _silu(proj(o_gb, o_end))


def _bcast_rows(g, tok):
    return jnp.broadcast_to(g.astype(F32)[:, None], (g.shape[0], tok))


def _even_pre(x_lat, x_ctx, mod, norm_g, w_in, a_g_cq, a_w_uq, a_g_ckv, a_w_ukv,
              a_g_qn, a_g_kn, b_g_qn, b_g_kn, rope_a, rope_b):
    n_lat, d = x_lat.shape
    t = ROW_TILE
    seq = n_lat + x_ctx.shape[0]
    n_tiles = seq // t
    e_in = w_in.shape[1]
    const2 = lambda i: (0, 0)
    tok2 = lambda i: (0, i)
    tok3 = lambda i: (0, 0, i)
    row3 = lambda i: (0, i, 0)
    in_specs = _token_specs(t, d, n_lat // t) + [
        pl.BlockSpec((8, 3 * d), const2),
        pl.BlockSpec((1, d), const2),
        pl.BlockSpec((e_in, d), const2),
        pl.BlockSpec((A_Q_LORA, t), const2),
        pl.BlockSpec((A_HEADS * A_QK, A_Q_LORA), const2),
        pl.BlockSpec((A_KV_LORA, t), const2),
        pl.BlockSpec((A_HEADS * (A_NOPE + A_V), A_KV_LORA), const2),
        pl.BlockSpec((A_QK, t), const2),
        pl.BlockSpec((A_QK, t), const2),
        pl.BlockSpec((A_ROPE // 2, t), tok2),
        pl.BlockSpec((A_ROPE // 2, t), tok2),
        pl.BlockSpec((B_D, t), const2),
        pl.BlockSpec((B_D, t), const2),
        pl.BlockSpec((B_D // 2, t), tok2),
        pl.BlockSpec((B_D // 2, t), tok2),
    ]
    out_shape = [
        jax.ShapeDtypeStruct((A_HEADS, HEAD_PAD, seq), BF16),
        jax.ShapeDtypeStruct((A_HEADS, seq, HEAD_PAD), BF16),
        jax.ShapeDtypeStruct((A_HEADS, 2 * A_V, seq), BF16),
        jax.ShapeDtypeStruct((A_HEADS * A_V, seq), F32),
        jax.ShapeDtypeStruct((A_HEADS, seq), F32),
        jax.ShapeDtypeStruct((A_HEADS, seq), F32),
        jax.ShapeDtypeStruct((2 * B_HEADS, HEAD_PAD, seq), BF16),
        jax.ShapeDtypeStruct((B_HEADS, seq, HEAD_PAD), BF16),
        jax.ShapeDtypeStruct((B_HEADS, B_V, seq), BF16),
        jax.ShapeDtypeStruct((B_HEADS * B_V, seq), F32),
        jax.ShapeDtypeStruct((2 * B_HEADS, seq), F32),
        jax.ShapeDtypeStruct((2 * B_HEADS, seq), F32),
    ]
    out_specs = [
        pl.BlockSpec((A_HEADS, HEAD_PAD, t), tok3),
        pl.BlockSpec((A_HEADS, t, HEAD_PAD), row3),
        pl.BlockSpec((A_HEADS, 2 * A_V, t), tok3),
        pl.BlockSpec((A_HEADS * A_V, t), tok2),
        pl.BlockSpec((A_HEADS, t), tok2),
        pl.BlockSpec((A_HEADS, t), tok2),
        pl.BlockSpec((2 * B_HEADS, HEAD_PAD, t), tok3),
        pl.BlockSpec((B_HEADS, t, HEAD_PAD), row3),
        pl.BlockSpec((B_HEADS, B_V, t), tok3),
        pl.BlockSpec((B_HEADS * B_V, t), tok2),
        pl.BlockSpec((2 * B_HEADS, t), tok2),
        pl.BlockSpec((2 * B_HEADS, t), tok2),
    ]
    return pl.pallas_call(
        functools.partial(_even_pre_kernel, n_lat // t),
        grid=(n_tiles,),
        in_specs=in_specs,
        out_specs=out_specs,
        out_shape=out_shape,
        compiler_params=_compiler_params(1),
        name="even_pre",
    )(x_lat, x_ctx, mod, norm_g.reshape(1, d), w_in.T.astype(BF16),
      _bcast_rows(a_g_cq, t), a_w_uq.T.astype(BF16),
      _bcast_rows(a_g_ckv, t), a_w_ukv.T.astype(BF16),
      _bcast_rows(a_g_qn, t), _bcast_rows(a_g_kn, t), rope_a[0], rope_a[1],
      _bcast_rows(b_g_qn, t), _bcast_rows(b_g_kn, t), rope_b[0], rope_b[1])


def _odd_pre_kernel(n_lat_tiles,
                    x_ref, xc_ref, mod_ref, g_ref, wt_ref, gq_ref, gk_ref, cos_ref, sin_ref,
                    qt_ref, k_ref, vt_ref, gt_ref, qn_ref, kn_ref):
    is_ctx = pl.program_id(0) >= n_lat_tiles
    hb = _modulated_rows(x_ref, xc_ref, mod_ref, g_ref, is_ctx)
    cos, sin = cos_ref[...], sin_ref[...]
    o_k = C_HEADS * C_HD
    o_v = o_k + C_KV_HEADS * C_HD
    o_g = o_v + C_KV_HEADS * C_HD
    o_end = o_g + C_HEADS * C_HD
    q_scale = (C_HD ** -0.5) * LOG2E
    q = _dot_nt(wt_ref[0:o_k, :], hb)
    norms = []
    for h in range(C_HEADS):
        qh = _rms_rows(q[h * C_HD:(h + 1) * C_HD], gq_ref[...])
        qt_ref[h], n2 = _round_with_sqnorm(_rope_rows(qh, cos, sin) * q_scale)
        norms.append(n2)
    qn_ref[...] = jnp.concatenate(norms, axis=0)
    kv = _dot_nt(wt_ref[o_k:o_g, :], hb)
    norms = []
    for h in range(C_KV_HEADS):
        kh = _rms_rows(kv[h * C_HD:(h + 1) * C_HD], gk_ref[...])
        kh = _rope_rows(kh, cos, sin)
        norms.append(_round_with_sqnorm(kh)[1])
        k_ref[h] = jnp.transpose(kh).astype(BF16)
        vt_ref[h] = kv[(C_KV_HEADS + h) * C_HD:(C_KV_HEADS + h + 1) * C_HD].astype(BF16)
    kn_ref[...] = jnp.concatenate(norms, axis=0)
    gt_ref[...] = _silu(_dot_nt(wt_ref[o_g:o_end, :], hb))


def _odd_pre(x_lat, x_ctx, mod, norm_g, w_in, g_qn, g_kn, rope_c):
    n_lat, d = x_lat.shape
    t = ROW_TILE
    seq = n_lat + x_ctx.shape[0]
    o_in = w_in.shape[1]
    const2 = lambda i: (0, 0)
    tok2 = lambda i: (0, i)
    tok3 = lambda i: (0, 0, i)
    in_specs = _token_specs(t, d, n_lat // t) + [
        pl.BlockSpec((8, 3 * d), const2),
        pl.BlockSpec((1, d), const2),
        pl.BlockSpec((o_in, d), const2),
        pl.BlockSpec((C_HD, t), const2),
        pl.BlockSpec((C_HD, t), const2),
        pl.BlockSpec((C_HD // 2, t), tok2),
        pl.BlockSpec((C_HD // 2, t), tok2),
    ]
    out_shape = [
        jax.ShapeDtypeStruct((C_HEADS, C_HD, seq), BF16),
        jax.ShapeDtypeStruct((C_KV_HEADS, seq, C_HD), BF16),
        jax.ShapeDtypeStruct((C_KV_HEADS, C_HD, seq), BF16),
        jax.ShapeDtypeStruct((C_HEADS * C_HD, seq), F32),
        jax.ShapeDtypeStruct((C_HEADS, seq), F32),
        jax.ShapeDtypeStruct((C_KV_HEADS, seq), F32),
    ]
    out_specs = [
        pl.BlockSpec((C_HEADS, C_HD, t), tok3),
        pl.BlockSpec((C_KV_HEADS, t, C_HD), lambda i: (0, i, 0)),
        pl.BlockSpec((C_KV_HEADS, C_HD, t), tok3),
        pl.BlockSpec((C_HEADS * C_HD, t), tok2),
        pl.BlockSpec((C_HEADS, t), tok2),
        pl.BlockSpec((C_KV_HEADS, t), tok2),
    ]
    return pl.pallas_call(
        functools.partial(_odd_pre_kernel, n_lat // t),
        grid=(seq // t,),
        in_specs=in_specs,
        out_specs=out_specs,
        out_shape=out_shape,
        compiler_params=_compiler_params(1),
        name="odd_pre",
    )(x_lat, x_ctx, mod, norm_g.reshape(1, d), w_in.T.astype(BF16),
      _bcast_rows(g_qn, t), _bcast_rows(g_kn, t), rope_c[0], rope_c[1])


def _flash_streams(qts, k_ref, vt_ref, s_ref, acc_ref, n_chunks, tk):
    n_s = len(qts)
    tq = qts[0].shape[1]
    for st in range(n_s):
        acc_ref[st] = jnp.zeros(acc_ref.shape[1:], F32)

    def scores(slot, c):
        start = pl.multiple_of(c * tk, tk)
        k = k_ref[0, pl.ds(start, tk), :]
        maxes = []
        for st in range(n_s):
            s = _dot(k, qts[st])
            s_ref[st, slot] = s
            maxes.append(jnp.max(s, axis=0, keepdims=True))
        return tuple(maxes)

    def consume(slot, c, state, maxes):
        ms, ls = state
        start = pl.multiple_of(c * tk, tk)
        vt = vt_ref[0, :, pl.ds(start, tk)]
        new_ms, new_ls = [], []
        for st in range(n_s):
            m_new = jnp.maximum(ms[st], maxes[st])
            alpha = jnp.exp2(ms[st] - m_new)
            p = jnp.exp2(s_ref[st, slot] - m_new)
            new_ls.append(alpha * ls[st] + jnp.sum(p, axis=0, keepdims=True))
            acc_ref[st] = alpha * acc_ref[st] + _dot(vt, p.astype(BF16))
            new_ms.append(m_new)
        return tuple(new_ms), tuple(new_ls)

    def step(c, slot, state, mx_cur):
        mx_next = scores(1 - slot, c + 1)
        return consume(slot, c, state, mx_cur), mx_next

    def group(j, carry):
        state, mx = carry
        for u in range(CHUNKS_PER_TRIP):
            state, mx = step(j * CHUNKS_PER_TRIP + u, u % 2, state, mx)
        return state, mx

    state = (tuple(jnp.full((1, tq), NEG_BIG, F32) for _ in range(n_s)),
             tuple(jnp.zeros((1, tq), F32) for _ in range(n_s)))
    mx = scores(0, 0)
    n_trips = (n_chunks - 1) // CHUNKS_PER_TRIP
    state, mx = lax.fori_loop(0, n_trips, group, (state, mx))
    for c in range(n_trips * CHUNKS_PER_TRIP, n_chunks - 1):
        state, mx = step(c, c % 2, state, mx)
    state = consume((n_chunks - 1) % 2, n_chunks - 1, state, mx)
    _, ls = state
    return [acc_ref[st] / ls[st] for st in range(n_s)]


def _bounded_streams(qts, bounds, k_ref, vt_ref, acc_ref, n_chunks, tk):
    n_s = len(qts)
    ls = [None] * n_s
    for c in range(n_chunks):
        k = k_ref[0, c * tk:(c + 1) * tk, :]
        vt = vt_ref[0, :, c * tk:(c + 1) * tk]
        for st in range(n_s):
            p = jnp.exp2(_dot(k, qts[st]) - bounds[st])
            p_sum = jnp.sum(p, axis=0, keepdims=True)
            pv = _dot(vt, p.astype(BF16))
            if c == 0:
                ls[st] = p_sum
                acc_ref[st] = pv
            else:
                ls[st] = ls[st] + p_sum
                acc_ref[st] += pv
    return [acc_ref[st] / ls[st] for st in range(n_s)]


def _attend(bounded, n_s, n_chunks, tk, refs):
    qts = [r[0] for r in refs[:n_s]]
    pos = n_s
    if bounded:
        bounds = [r[0] for r in refs[pos:pos + n_s]]
        pos += n_s
    k_ref, vt_ref = refs[pos:pos + 2]
    pos += 2
    if bounded:
        yt_ref, acc_ref = refs[-2:]
        outs = _bounded_streams(qts, bounds, k_ref, vt_ref, acc_ref, n_chunks, tk)
        return outs, refs[pos:-2], yt_ref
    yt_ref, s_ref, acc_ref = refs[-3:]
    outs = _flash_streams(qts, k_ref, vt_ref, s_ref, acc_ref, n_chunks, tk)
    return outs, refs[pos:-3], yt_ref


def _flash_gate_kernel(bounded, n_chunks, tk, *refs):
    (o,), (gate_ref,), yt_ref = _attend(bounded, 1, n_chunks, tk, refs)
    yt_ref[...] = (o[:gate_ref.shape[0]] * gate_ref[...]).astype(BF16)


def _flash_diff_kernel(lam_init, bounded, n_chunks, tk, *refs):
    (o1, o2), (gate_ref, lam_ref, gout_ref), yt_ref = _attend(bounded, 2, n_chunks, tk, refs)
    lv = lam_ref[...]
    lam = (jnp.exp(jnp.sum(lv[0:1] * lv[1:2], axis=-1, keepdims=True))
           - jnp.exp(jnp.sum(lv[2:3] * lv[3:4], axis=-1, keepdims=True)) + lam_init)
    o = o1 - lam * o2
    o = _rms_rows(o, gout_ref[...]) * (1.0 - lam_init)
    yt_ref[...] = (o * gate_ref[...]).astype(BF16)


def _flash_call(kernel_fn, n_streams, qt, bound, k, vt, gate, extra, *, n_heads, kv_group,
                q_tiles, q_tile0, tq, k_len, k_blk0, tk, name):
    dv = vt.shape[1]
    dv_out = gate.shape[0] // n_heads
    n_chunks = k_len // tk
    bounded = bound is not None

    def stream_specs(rows):
        return [
            pl.BlockSpec((1, rows, tq),
                         functools.partial(lambda s, h, i: (n_streams * h + s, 0, q_tile0 + i), s))
            for s in range(n_streams)
        ]

    in_specs = stream_specs(HEAD_PAD) + (stream_specs(1) if bounded else []) + [
        pl.BlockSpec((1, k_len, HEAD_PAD), lambda h, i: (h // kv_group, k_blk0, 0)),
        pl.BlockSpec((1, dv, k_len), lambda h, i: (h // kv_group, 0, k_blk0)),
        pl.BlockSpec((dv_out, tq), lambda h, i: (h, q_tile0 + i)),
    ] + [pl.BlockSpec(e.shape, lambda h, i: (0, 0)) for e in extra]
    scratch = [pltpu.VMEM((n_streams, dv, tq), F32)]
    if not bounded:
        scratch = [pltpu.VMEM((n_streams, 2, tk, tq), F32)] + scratch
    args = [qt] * n_streams + ([bound] * n_streams if bounded else [])
    return pl.pallas_call(
        functools.partial(kernel_fn, bounded, n_chunks, tk),
        grid=(n_heads, q_tiles),
        in_specs=in_specs,
        out_specs=pl.BlockSpec((dv_out, tq), lambda h, i: (h, i)),
        out_shape=jax.ShapeDtypeStruct((n_heads * dv_out, q_tiles * tq), BF16),
        scratch_shapes=scratch,
        compiler_params=_compiler_params(2),
        name=name,
    )(*args, k, vt, gate, *extra)


def _prefix_flash(kernel_fn, n_streams, qt, k, vt, gate, qn2, kn2, extra, *, n_heads,
                  kv_group, n_lat, n_ctx, update_ctx, name):
    seq = n_lat + n_ctx
    tq = 512 if n_lat % 512 == 0 else ROW_TILE
    if name == "flash_gqa" and n_lat % 1024 == 0:
        tq = 1024
    tk = 1280 if seq % 1280 == 0 else ROW_TILE
    common = dict(n_heads=n_heads, kv_group=kv_group)
    lat = dict(q_tiles=n_lat // tq, q_tile0=0, tq=tq, k_len=seq, k_blk0=0, tk=tk, **common)

    k_max2 = jnp.repeat(jnp.max(kn2, axis=1), qn2.shape[0] // kn2.shape[0])
    bound = jnp.sqrt(qn2[:, :n_lat] * k_max2[:, None]) * BOUND_MARGIN
    ex = extra(tq)

    def bounded(*a):
        return _flash_call(kernel_fn, n_streams, a[0], a[1], *a[2:5], a[5:], name=name, **lat)

    def online(*a):
        return _flash_call(kernel_fn, n_streams, a[0], None, *a[2:5], a[5:],
                           name=name + "_online", **lat)

    y = lax.cond(jnp.max(bound) <= SCORE_BOUND_LIMIT, bounded, online,
                 qt, bound[:, None, :], k, vt, gate, *ex)
    if not update_ctx:
        return y, None
    y_ctx = _flash_call(kernel_fn, n_streams, qt, None, k, vt, gate, extra(n_ctx),
                        q_tiles=1, q_tile0=n_lat // n_ctx, tq=n_ctx,
                        k_len=n_ctx, k_blk0=n_lat // n_ctx, tk=n_ctx,
                        name=name + "_ctx", **common)
    return y, y_ctx


def _post_kernel(mod_row, n_y, *refs):
    y_refs = refs[:n_y]
    wo_ref, x_ref, mod_ref, o_ref = refs[n_y:]
    d = x_ref.shape[-1]
    acc = None
    off = 0
    for y_ref in y_refs:
        w = y_ref.shape[0]
        part = _dot(wo_ref[:, off:off + w], y_ref[...])
        acc = part if acc is None else acc + part
        off += w
    gate = mod_ref[mod_row:mod_row + 1, 2 * d:3 * d]
    o_ref[...] = x_ref[...] + gate * jnp.transpose(acc)


def _post(ys, w_out_t, x, mod, mod_row, name):
    n_rows, d = x.shape
    t = POST_TILE if n_rows % POST_TILE == 0 else ROW_TILE
    width = w_out_t.shape[1]
    in_specs = [pl.BlockSpec((y.shape[0], t), lambda i: (0, i)) for y in ys] + [
        pl.BlockSpec((d, width), lambda i: (0, 0)),
        pl.BlockSpec((t, d), lambda i: (i, 0)),
        pl.BlockSpec((8, 3 * d), lambda i: (0, 0)),
    ]
    return pl.pallas_call(
        functools.partial(_post_kernel, mod_row, len(ys)),
        grid=(n_rows // t,),
        in_specs=in_specs,
        out_specs=pl.BlockSpec((t, d), lambda i: (i, 0)),
        out_shape=jax.ShapeDtypeStruct((n_rows, d), F32),
        compiler_params=_compiler_params(1),
        name=name,
    )(*ys, w_out_t, x, mod)


def _rope_tables(n_lat, n_ctx, dim):
    tok = np.arange(n_lat)
    rows, cols = tok // GRID_W, tok % GRID_W
    n_freq = dim // 4
    inv = ROPE_THETA ** (-np.arange(n_freq) / n_freq)
    ang = np.concatenate([rows[:, None] * inv, cols[:, None] * inv], axis=-1)
    ang = np.concatenate([ang, np.zeros((n_ctx, dim // 2))], axis=0)
    return (jnp.asarray(np.cos(ang).T, dtype=F32), jnp.asarray(np.sin(ang).T, dtype=F32))


def kernel(x, c, ctx, c_ctx, e_norm_g, e_w_mod, e_b_mod, e_w_in, a_g_cq, a_w_uq, a_g_ckv, a_w_ukv, a_g_qn, a_g_kn, b_g_qn, b_g_kn, b_lam_q1, b_lam_k1, b_lam_q2, b_lam_k2, b_g_out, e_w_out, o_norm_g, o_w_mod, o_b_mod, o_w_in, c_g_qn, c_g_kn, o_w_out):
    batch, n_lat, d = x.shape
    n_ctx = ctx.shape[1]
    assert batch == 1 and n_ctx == ROW_TILE and n_lat % ROW_TILE == 0
    depth = e_norm_g.shape[0] + o_norm_g.shape[0]

    cvec = jnp.zeros((8, d), F32).at[0].set(c[0]).at[1].set(c_ctx)
    rope_a = _rope_tables(n_lat, n_ctx, A_ROPE)
    rope_b = _rope_tables(n_lat, n_ctx, B_D)
    rope_c = _rope_tables(n_lat, n_ctx, C_HD)

    x_lat, x_ctx = x[0], ctx[0]
    for layer in range(depth):
        i = layer // 2
        update_ctx = layer < depth - 1
        pf = dict(n_lat=n_lat, n_ctx=n_ctx, update_ctx=update_ctx)
        if layer % 2 == 0:
            lam_init = 0.8 - 0.6 * math.exp(-0.3 * layer)
            mod = _modulation(cvec, e_w_mod[i], e_b_mod[i])
            qta, ka, vta, gta, qna, kna, qtb, kb, vtb, gtb, qnb, knb = _even_pre(
                x_lat, x_ctx, mod, e_norm_g[i], e_w_in[i], a_g_cq[i], a_w_uq[i],
                a_g_ckv[i], a_w_ukv[i], a_g_qn[i], a_g_kn[i], b_g_qn[i], b_g_kn[i],
                rope_a, rope_b)
            ya = _prefix_flash(_flash_gate_kernel, 1, qta, ka, vta, gta, qna, kna,
                               lambda tq: [], n_heads=A_HEADS, kv_group=1,
                               name="flash_mla", **pf)
            lam_vecs = jnp.stack([b_lam_q1[i], b_lam_k1[i], b_lam_q2[i], b_lam_k2[i]]).astype(F32)
            yb = _prefix_flash(
                functools.partial(_flash_diff_kernel, lam_init), 2, qtb, kb, vtb, gtb, qnb, knb,
                lambda tq: [lam_vecs, _bcast_rows(b_g_out[i], tq)],
                n_heads=B_HEADS, kv_group=1, name="flash_diff", **pf)
            ys, w_out = [ya, yb], e_w_out[i]
        else:
            mod = _modulation(cvec, o_w_mod[i], o_b_mod[i])
            qtc, kc, vtc, gtc, qnc, knc = _odd_pre(
                x_lat, x_ctx, mod, o_norm_g[i], o_w_in[i], c_g_qn[i], c_g_kn[i], rope_c)
            ys = [_prefix_flash(_flash_gate_kernel, 1, qtc, kc, vtc, gtc, qnc, knc,
                                lambda tq: [], n_heads=C_HEADS,
                                kv_group=C_HEADS // C_KV_HEADS, name="flash_gqa", **pf)]
            w_out = o_w_out[i]
        w_out_t = w_out.T.astype(BF16)
        if update_ctx:
            x_ctx = _post([y[1] for y in ys], w_out_t, x_ctx, mod, 1, "post_ctx")
        x_lat = _post([y[0] for y in ys], w_out_t, x_lat, mod, 0, "post")
    return x_lat.reshape(batch, n_lat, d)
```

```python
import functools
import math

import jax
import jax.numpy as jnp
import numpy as np
from jax import lax
from jax.experimental import pallas as pl
from jax.experimental.pallas import tpu as pltpu

F32 = jnp.float32
BF16 = jnp.bfloat16

GRID_W = 64
ROPE_THETA = 10000.0
EPS = 1e-6
LOG2E = math.log2(math.e)

A_HEADS, A_Q_LORA, A_KV_LORA, A_NOPE, A_ROPE, A_V = 8, 256, 128, 64, 32, 64
A_QK = A_NOPE + A_ROPE
B_HEADS, B_D = 4, 64
B_V = 2 * B_D
C_HEADS, C_KV_HEADS, C_HD = 8, 2, 128

HEAD_PAD = 128
ROW_TILE = 256
POST_TILE = 512
FLASH_QUERY_LANES = 1024
CHUNKS_PER_TRIP = 4
NEG_BIG = -1e30
SCORE_BOUND_LIMIT = 48.0
BOUND_MARGIN = 1.0 + 2.0 ** -8
VMEM_LIMIT_BYTES = 56 * 1024 * 1024


def _silu(v):
    return v / (1.0 + jnp.exp(-v))


def _dot(a, b):
    return jnp.dot(a, b, preferred_element_type=F32)


def _dot_nt(a, b):
    return lax.dot_general(a, b, (((1,), (1,)), ((), ())), preferred_element_type=F32)


def _rms_rows(v, g):
    ms = jnp.mean(v * v, axis=0, keepdims=True)
    return v * lax.rsqrt(ms + EPS) * g


def _round_with_sqnorm(v):
    vb = v.astype(BF16)
    vf = vb.astype(F32)
    return vb, jnp.sum(vf * vf, axis=0, keepdims=True)


def _rope_rows(v, cos, sin):
    half = v.shape[0] // 2
    x1, x2 = v[:half], v[half:]
    return jnp.concatenate([x1 * cos - x2 * sin, x2 * cos + x1 * sin], axis=0)


def _compiler_params(n_grid):
    return pltpu.CompilerParams(
        dimension_semantics=("arbitrary",) * n_grid,
        vmem_limit_bytes=VMEM_LIMIT_BYTES,
    )


def _mod_kernel(c_ref, w_ref, b_ref, o_ref):
    a = _silu(c_ref[...])
    w = w_ref[...]
    a_hi = a.astype(BF16)
    a_lo = (a - a_hi.astype(F32)).astype(BF16)
    w_hi = w.astype(BF16)
    w_lo = (w - w_hi.astype(F32)).astype(BF16)
    acc = _dot(a_hi, w_hi) + _dot(a_hi, w_lo) + _dot(a_lo, w_hi)
    o_ref[...] = acc + b_ref[...]


def _modulation(cvec, w_mod, b_mod):
    d, d3 = w_mod.shape
    col = 768
    return pl.pallas_call(
        _mod_kernel,
        grid=(d3 // col,),
        in_specs=[
            pl.BlockSpec((8, d), lambda j: (0, 0)),
            pl.BlockSpec((d, col), lambda j: (0, j)),
            pl.BlockSpec((1, col), lambda j: (0, j)),
        ],
        out_specs=pl.BlockSpec((8, col), lambda j: (0, j)),
        out_shape=jax.ShapeDtypeStruct((8, d3), F32),
        compiler_params=_compiler_params(1),
        name="modulation",
    )(cvec, w_mod, b_mod.reshape(1, d3))


def _token_specs(t, d, n_lat_tiles):
    return [pl.BlockSpec((t, d), lambda i: (jnp.minimum(i, n_lat_tiles - 1), 0)),
            pl.BlockSpec((t, d), lambda i: (0, 0))]


def _modulated_rows(x_ref, xc_ref, mod_ref, g_ref, is_ctx):
    d = x_ref.shape[-1]
    x = jnp.where(is_ctx, xc_ref[...], x_ref[...])
    sh = jnp.where(is_ctx, mod_ref[1:2, 0:d], mod_ref[0:1, 0:d])
    sc = jnp.where(is_ctx, mod_ref[1:2, d:2 * d], mod_ref[0:1, d:2 * d])
    ms = jnp.mean(x * x, axis=-1, keepdims=True)
    y = x * lax.rsqrt(ms + EPS) * g_ref[...]
    return (y * (1.0 + sc) + sh).astype(BF16)


def _even_pre_kernel(n_lat_tiles,
                     x_ref, xc_ref, mod_ref, g_ref, wt_ref,
                     gcq_ref, wuq_ref, gckv_ref, wukv_ref, gqa_ref, gka_ref,
                     cosa_ref, sina_ref, gqb_ref, gkb_ref, cosb_ref, sinb_ref,
                     qta_ref, ka_ref, vta_ref, gta_ref, qna_ref, kna_ref,
                     qtb_ref, kb_ref, vtb_ref, gtb_ref, qnb_ref, knb_ref):
    is_ctx = pl.program_id(0) >= n_lat_tiles
    hb = _modulated_rows(x_ref, xc_ref, mod_ref, g_ref, is_ctx)
    tok = hb.shape[0]

    o_cq, o_ckv, o_kr = 0, A_Q_LORA, A_Q_LORA + A_KV_LORA
    o_ga = o_kr + A_ROPE
    o_bq = o_ga + A_HEADS * A_V
    o_bk = o_bq + 2 * B_HEADS * B_D
    o_bv = o_bk + 2 * B_HEADS * B_D
    o_gb = o_bv + B_HEADS * B_V
    o_end = o_gb + B_HEADS * B_V

    def proj(lo, hi):
        return _dot_nt(wt_ref[lo:hi, :], hb)

    cq = _rms_rows(proj(o_cq, o_ckv), gcq_ref[...]).astype(BF16)
    q_all = _dot(wuq_ref[...], cq)
    cos_a, sin_a = cosa_ref[...], sina_ref[...]
    q_scale = (A_QK ** -0.5) * LOG2E
    pad_a = jnp.zeros((HEAD_PAD - A_QK, tok), F32)
    norms = []
    for h in range(A_HEADS):
        qh = _rms_rows(q_all[h * A_QK:(h + 1) * A_QK], gqa_ref[...])
        qh = jnp.concatenate(
            [qh[:A_NOPE], _rope_rows(qh[A_NOPE:], cos_a, sin_a), pad_a], axis=0)
        qta_ref[h], n2 = _round_with_sqnorm(qh * q_scale)
        norms.append(n2)
    qna_ref[...] = jnp.concatenate(norms, axis=0)

    ckv_kr = proj(o_ckv, o_ga)
    ckv = _rms_rows(ckv_kr[:A_KV_LORA], gckv_ref[...]).astype(BF16)
    kr = ckv_kr[A_KV_LORA:]
    kv_all = _dot(wukv_ref[...], ckv)
    norms = []
    for h in range(A_HEADS):
        base = h * (A_NOPE + A_V)
        kh = jnp.concatenate([kv_all[base:base + A_NOPE], kr], axis=0)
        kh = _rms_rows(kh, gka_ref[...])
        kh = jnp.concatenate(
            [kh[:A_NOPE], _rope_rows(kh[A_NOPE:], cos_a, sin_a), pad_a], axis=0)
        norms.append(_round_with_sqnorm(kh)[1])
        ka_ref[h] = jnp.transpose(kh).astype(BF16)
        vh = kv_all[base + A_NOPE:base + A_NOPE + A_V]
        vta_ref[h] = jnp.concatenate([vh, jnp.zeros_like(vh)], axis=0).astype(BF16)
    kna_ref[...] = jnp.concatenate(norms, axis=0)

    gta_ref[...] = _silu(proj(o_ga, o_bq))

    cos_b, sin_b = cosb_ref[...], sinb_ref[...]
    qb_scale = (B_D ** -0.5) * LOG2E
    pad_b = jnp.zeros((B_D, tok), F32)
    bq = proj(o_bq, o_bk)
    bk = proj(o_bk, o_bv)
    q_norms, k_norms = [], []
    for h in range(B_HEADS):
        ks = []
        for j in range(2):
            m = 2 * h + j
            qm = _rms_rows(bq[m * B_D:(m + 1) * B_D], gqb_ref[...])
            qm = _rope_rows(qm, cos_b, sin_b) * qb_scale
            parts = [qm, pad_b] if j == 0 else [pad_b, qm]
            qtb_ref[m], n2 = _round_with_sqnorm(jnp.concatenate(parts, axis=0))
            q_norms.append(n2)
            km = _rms_rows(bk[m * B_D:(m + 1) * B_D], gkb_ref[...])
            km = _rope_rows(km, cos_b, sin_b)
            k_norms.append(_round_with_sqnorm(km)[1])
            ks.append(km)
        kb_ref[h] = jnp.transpose(jnp.concatenate(ks, axis=0)).astype(BF16)
    qnb_ref[...] = jnp.concatenate(q_norms, axis=0)
    knb_ref[...] = jnp.concatenate(k_norms, axis=0)
    bv = proj(o_bv, o_gb)
    for h in range(B_HEADS):
        vtb_ref[h] = bv[h * B_V:(h + 1) * B_V].astype(BF16)
    gtb_ref[...] = _silu(proj(o_gb, o_end))


def _bcast_rows(g, tok):
    return jnp.broadcast_to(g.astype(F32)[:, None], (g.shape[0], tok))


def _even_pre(x_lat, x_ctx, mod, norm_g, w_in, a_g_cq, a_w_uq, a_g_ckv, a_w_ukv,
              a_g_qn, a_g_kn, b_g_qn, b_g_kn, rope_a, rope_b):
    n_lat, d = x_lat.shape
    t = ROW_TILE
    seq = n_lat + x_ctx.shape[0]
    n_tiles = seq // t
    e_in = w_in.shape[1]
    const2 = lambda i: (0, 0)
    tok2 = lambda i: (0, i)
    tok3 = lambda i: (0, 0, i)
    row3 = lambda i: (0, i, 0)
    in_specs = _token_specs(t, d, n_lat // t) + [
        pl.BlockSpec((8, 3 * d), const2),
        pl.BlockSpec((1, d), const2),
        pl.BlockSpec((e_in, d), const2),
        pl.BlockSpec((A_Q_LORA, t), const2),
        pl.BlockSpec((A_HEADS * A_QK, A_Q_LORA), const2),
        pl.BlockSpec((A_KV_LORA, t), const2),
        pl.BlockSpec((A_HEADS * (A_NOPE + A_V), A_KV_LORA), const2),
        pl.BlockSpec((A_QK, t), const2),
        pl.BlockSpec((A_QK, t), const2),
        pl.BlockSpec((A_ROPE // 2, t), tok2),
        pl.BlockSpec((A_ROPE // 2, t), tok2),
        pl.BlockSpec((B_D, t), const2),
        pl.BlockSpec((B_D, t), const2),
        pl.BlockSpec((B_D // 2, t), tok2),
        pl.BlockSpec((B_D // 2, t), tok2),
    ]
    out_shape = [
        jax.ShapeDtypeStruct((A_HEADS, HEAD_PAD, seq), BF16),
        jax.ShapeDtypeStruct((A_HEADS, seq, HEAD_PAD), BF16),
        jax.ShapeDtypeStruct((A_HEADS, 2 * A_V, seq), BF16),
        jax.ShapeDtypeStruct((A_HEADS * A_V, seq), F32),
        jax.ShapeDtypeStruct((A_HEADS, seq), F32),
        jax.ShapeDtypeStruct((A_HEADS, seq), F32),
        jax.ShapeDtypeStruct((2 * B_HEADS, HEAD_PAD, seq), BF16),
        jax.ShapeDtypeStruct((B_HEADS, seq, HEAD_PAD), BF16),
        jax.ShapeDtypeStruct((B_HEADS, B_V, seq), BF16),
        jax.ShapeDtypeStruct((B_HEADS * B_V, seq), F32),
        jax.ShapeDtypeStruct((2 * B_HEADS, seq), F32),
        jax.ShapeDtypeStruct((2 * B_HEADS, seq), F32),
    ]
    out_specs = [
        pl.BlockSpec((A_HEADS, HEAD_PAD, t), tok3),
        pl.BlockSpec((A_HEADS, t, HEAD_PAD), row3),
        pl.BlockSpec((A_HEADS, 2 * A_V, t), tok3),
        pl.BlockSpec((A_HEADS * A_V, t), tok2),
        pl.BlockSpec((A_HEADS, t), tok2),
        pl.BlockSpec((A_HEADS, t), tok2),
        pl.BlockSpec((2 * B_HEADS, HEAD_PAD, t), tok3),
        pl.BlockSpec((B_HEADS, t, HEAD_PAD), row3),
        pl.BlockSpec((B_HEADS, B_V, t), tok3),
        pl.BlockSpec((B_HEADS * B_V, t), tok2),
        pl.BlockSpec((2 * B_HEADS, t), tok2),
        pl.BlockSpec((2 * B_HEADS, t), tok2),
    ]
    return pl.pallas_call(
        functools.partial(_even_pre_kernel, n_lat // t),
        grid=(n_tiles,),
        in_specs=in_specs,
        out_specs=out_specs,
        out_shape=out_shape,
        compiler_params=_compiler_params(1),
        name="even_pre",
    )(x_lat, x_ctx, mod, norm_g.reshape(1, d), w_in.T.astype(BF16),
      _bcast_rows(a_g_cq, t), a_w_uq.T.astype(BF16),
      _bcast_rows(a_g_ckv, t), a_w_ukv.T.astype(BF16),
      _bcast_rows(a_g_qn, t), _bcast_rows(a_g_kn, t), rope_a[0], rope_a[1],
      _bcast_rows(b_g_qn, t), _bcast_rows(b_g_kn, t), rope_b[0], rope_b[1])


def _odd_pre_kernel(n_lat_tiles,
                    x_ref, xc_ref, mod_ref, g_ref, wt_ref, gq_ref, gk_ref, cos_ref, sin_ref,
                    qt_ref, k_ref, vt_ref, gt_ref, qn_ref, kn_ref):
    is_ctx = pl.program_id(0) >= n_lat_tiles
    hb = _modulated_rows(x_ref, xc_ref, mod_ref, g_ref, is_ctx)
    cos, sin = cos_ref[...], sin_ref[...]
    o_k = C_HEADS * C_HD
    o_v = o_k + C_KV_HEADS * C_HD
    o_g = o_v + C_KV_HEADS * C_HD
    o_end = o_g + C_HEADS * C_HD
    q_scale = (C_HD ** -0.5) * LOG2E
    q = _dot_nt(wt_ref[0:o_k, :], hb)
    norms = []
    for h in range(C_HEADS):
        qh = _rms_rows(q[h * C_HD:(h + 1) * C_HD], gq_ref[...])
        qt_ref[h], n2 = _round_with_sqnorm(_rope_rows(qh, cos, sin) * q_scale)
        norms.append(n2)
    qn_ref[...] = jnp.concatenate(norms, axis=0)
    kv = _dot_nt(wt_ref[o_k:o_g, :], hb)
    norms = []
    for h in range(C_KV_HEADS):
        kh = _rms_rows(kv[h * C_HD:(h + 1) * C_HD], gk_ref[...])
        kh = _rope_rows(kh, cos, sin)
        norms.append(_round_with_sqnorm(kh)[1])
        k_ref[h] = jnp.transpose(kh).astype(BF16)
        vt_ref[h] = kv[(C_KV_HEADS + h) * C_HD:(C_KV_HEADS + h + 1) * C_HD].astype(BF16)
    kn_ref[...] = jnp.concatenate(norms, axis=0)
    gt_ref[...] = _silu(_dot_nt(wt_ref[o_g:o_end, :], hb))


def _odd_pre(x_lat, x_ctx, mod, norm_g, w_in, g_qn, g_kn, rope_c):
    n_lat, d = x_lat.shape
    t = ROW_TILE
    seq = n_lat + x_ctx.shape[0]
    o_in = w_in.shape[1]
    const2 = lambda i: (0, 0)
    tok2 = lambda i: (0, i)
    tok3 = lambda i: (0, 0, i)
    in_specs = _token_specs(t, d, n_lat // t) + [
        pl.BlockSpec((8, 3 * d), const2),
        pl.BlockSpec((1, d), const2),
        pl.BlockSpec((o_in, d), const2),
        pl.BlockSpec((C_HD, t), const2),
        pl.BlockSpec((C_HD, t), const2),
        pl.BlockSpec((C_HD // 2, t), tok2),
        pl.BlockSpec((C_HD // 2, t), tok2),
    ]
    out_shape = [
        jax.ShapeDtypeStruct((C_HEADS, C_HD, seq), BF16),
        jax.ShapeDtypeStruct((C_KV_HEADS, seq, C_HD), BF16),
        jax.ShapeDtypeStruct((C_KV_HEADS, C_HD, seq), BF16),
        jax.ShapeDtypeStruct((C_HEADS * C_HD, seq), F32),
        jax.ShapeDtypeStruct((C_HEADS, seq), F32),
        jax.ShapeDtypeStruct((C_KV_HEADS, seq), F32),
    ]
    out_specs = [
        pl.BlockSpec((C_HEADS, C_HD, t), tok3),
        pl.BlockSpec((C_KV_HEADS, t, C_HD), lambda i: (0, i, 0)),
        pl.BlockSpec((C_KV_HEADS, C_HD, t), tok3),
        pl.BlockSpec((C_HEADS * C_HD, t), tok2),
        pl.BlockSpec((C_HEADS, t), tok2),
        pl.BlockSpec((C_KV_HEADS, t), tok2),
    ]
    return pl.pallas_call(
        functools.partial(_odd_pre_kernel, n_lat // t),
        grid=(seq // t,),
        in_specs=in_specs,
        out_specs=out_specs,
        out_shape=out_shape,
        compiler_params=_compiler_params(1),
        name="odd_pre",
    )(x_lat, x_ctx, mod, norm_g.reshape(1, d), w_in.T.astype(BF16),
      _bcast_rows(g_qn, t), _bcast_rows(g_kn, t), rope_c[0], rope_c[1])


def _flash_streams(qts, k_ref, vt_ref, s_ref, acc_ref, n_chunks, tk):
    n_s = len(qts)
    tq = qts[0].shape[1]
    for st in range(n_s):
        acc_ref[st] = jnp.zeros(acc_ref.shape[1:], F32)

    def scores(slot, c):
        start = pl.multiple_of(c * tk, tk)
        k = k_ref[0, pl.ds(start, tk), :]
        maxes = []
        for st in range(n_s):
            s = _dot(k, qts[st])
            s_ref[st, slot] = s
            maxes.append(jnp.max(s, axis=0, keepdims=True))
        return tuple(maxes)

    def consume(slot, c, state, maxes):
        ms, ls = state
        start = pl.multiple_of(c * tk, tk)
        vt = vt_ref[0, :, pl.ds(start, tk)]
        new_ms, new_ls = [], []
        for st in range(n_s):
            m_new = jnp.maximum(ms[st], maxes[st])
            alpha = jnp.exp2(ms[st] - m_new)
            p = jnp.exp2(s_ref[st, slot] - m_new)
            new_ls.append(alpha * ls[st] + jnp.sum(p, axis=0, keepdims=True))
            acc_ref[st] = alpha * acc_ref[st] + _dot(vt, p.astype(BF16))
            new_ms.append(m_new)
        return tuple(new_ms), tuple(new_ls)

    def step(c, slot, state, mx_cur):
        mx_next = scores(1 - slot, c + 1)
        return consume(slot, c, state, mx_cur), mx_next

    def group(j, carry):
        state, mx = carry
        for u in range(CHUNKS_PER_TRIP):
            state, mx = step(j * CHUNKS_PER_TRIP + u, u % 2, state, mx)
        return state, mx

    state = (tuple(jnp.full((1, tq), NEG_BIG, F32) for _ in range(n_s)),
             tuple(jnp.zeros((1, tq), F32) for _ in range(n_s)))
    mx = scores(0, 0)
    n_trips = (n_chunks - 1) // CHUNKS_PER_TRIP
    state, mx = lax.fori_loop(0, n_trips, group, (state, mx))
    for c in range(n_trips * CHUNKS_PER_TRIP, n_chunks - 1):
        state, mx = step(c, c % 2, state, mx)
    state = consume((n_chunks - 1) % 2, n_chunks - 1, state, mx)
    _, ls = state
    return [acc_ref[st] / ls[st] for st in range(n_s)]


def _bounded_streams(qts, bounds, k_ref, vt_ref, acc_ref, n_chunks, tk):
    n_s = len(qts)
    tq = qts[0].shape[1]
    q_all = jnp.concatenate(qts, axis=1) if n_s > 1 else qts[0]
    b_all = jnp.concatenate(bounds, axis=1) if n_s > 1 else bounds[0]
    l_all = None
    for c in range(n_chunks):
        k = k_ref[0, c * tk:(c + 1) * tk, :]
        vt = vt_ref[0, :, c * tk:(c + 1) * tk]
        p = jnp.exp2(_dot(k, q_all) - b_all)
        p_sum = jnp.sum(p, axis=0, keepdims=True)
        pv = _dot(vt, p.astype(BF16))
        if c == 0:
            l_all = p_sum
            acc_ref[...] = pv
        else:
            l_all = l_all + p_sum
            acc_ref[...] += pv
    o_all = acc_ref[...] / l_all
    return [o_all[:, st * tq:(st + 1) * tq] for st in range(n_s)]


def _attend(bounded, n_s, n_chunks, tk, refs):
    qts = [r[0] for r in refs[:n_s]]
    pos = n_s
    if bounded:
        bounds = [r[0] for r in refs[pos:pos + n_s]]
        pos += n_s
    k_ref, vt_ref = refs[pos:pos + 2]
    pos += 2
    if bounded:
        yt_ref, acc_ref = refs[-2:]
        outs = _bounded_streams(qts, bounds, k_ref, vt_ref, acc_ref, n_chunks, tk)
        return outs, refs[pos:-2], yt_ref
    yt_ref, s_ref, acc_ref = refs[-3:]
    outs = _flash_streams(qts, k_ref, vt_ref, s_ref, acc_ref, n_chunks, tk)
    return outs, refs[pos:-3], yt_ref


def _flash_gate_kernel(bounded, n_chunks, tk, *refs):
    (o,), (gate_ref,), yt_ref = _attend(bounded, 1, n_chunks, tk, refs)
    yt_ref[...] = (o[:gate_ref.shape[0]] * gate_ref[...]).astype(BF16)


def _flash_diff_kernel(lam_init, bounded, n_chunks, tk, *refs):
    (o1, o2), (gate_ref, lam_ref, gout_ref), yt_ref = _attend(bounded, 2, n_chunks, tk, refs)
    lv = lam_ref[...]
    lam = (jnp.exp(jnp.sum(lv[0:1] * lv[1:2], axis=-1, keepdims=True))
           - jnp.exp(jnp.sum(lv[2:3] * lv[3:4], axis=-1, keepdims=True)) + lam_init)
    o = o1 - lam * o2
    o = _rms_rows(o, gout_ref[...]) * (1.0 - lam_init)
    yt_ref[...] = (o * gate_ref[...]).astype(BF16)


def _flash_call(kernel_fn, n_streams, qt, bound, k, vt, gate, extra, *, n_heads, kv_group,
                q_tiles, q_tile0, tq, k_len, k_blk0, tk, name):
    dv = vt.shape[1]
    dv_out = gate.shape[0] // n_heads
    n_chunks = k_len // tk
    bounded = bound is not None

    def stream_specs(rows):
        return [
            pl.BlockSpec((1, rows, tq),
                         functools.partial(lambda s, h, i: (n_streams * h + s, 0, q_tile0 + i), s))
            for s in range(n_streams)
        ]

    in_specs = stream_specs(HEAD_PAD) + (stream_specs(1) if bounded else []) + [
        pl.BlockSpec((1, k_len, HEAD_PAD), lambda h, i: (h // kv_group, k_blk0, 0)),
        pl.BlockSpec((1, dv, k_len), lambda h, i: (h // kv_group, 0, k_blk0)),
        pl.BlockSpec((dv_out, tq), lambda h, i: (h, q_tile0 + i)),
    ] + [pl.BlockSpec(e.shape, lambda h, i: (0, 0)) for e in extra]
    if bounded:
        scratch = [pltpu.VMEM((dv, n_streams * tq), F32)]
    else:
        scratch = [pltpu.VMEM((n_streams, 2, tk, tq), F32),
                   pltpu.VMEM((n_streams, dv, tq), F32)]
    args = [qt] * n_streams + ([bound] * n_streams if bounded else [])
    return pl.pallas_call(
        functools.partial(kernel_fn, bounded, n_chunks, tk),
        grid=(n_heads, q_tiles),
        in_specs=in_specs,
        out_specs=pl.BlockSpec((dv_out, tq), lambda h, i: (h, i)),
        out_shape=jax.ShapeDtypeStruct((n_heads * dv_out, q_tiles * tq), BF16),
        scratch_shapes=scratch,
        compiler_params=_compiler_params(2),
        name=name,
    )(*args, k, vt, gate, *extra)


def _prefix_flash(kernel_fn, n_streams, qt, k, vt, gate, qn2, kn2, extra, *, n_heads,
                  kv_group, n_lat, n_ctx, update_ctx, name):
    seq = n_lat + n_ctx
    tq = FLASH_QUERY_LANES // n_streams
    if n_lat % tq != 0:
        tq = ROW_TILE
    tk = 1280 if seq % 1280 == 0 else ROW_TILE
    common = dict(n_heads=n_heads, kv_group=kv_group)
    lat = dict(q_tiles=n_lat // tq, q_tile0=0, tq=tq, k_len=seq, k_blk0=0, tk=tk, **common)

    k_max2 = jnp.repeat(jnp.max(kn2, axis=1), qn2.shape[0] // kn2.shape[0])
    bound = jnp.sqrt(qn2[:, :n_lat] * k_max2[:, None]) * BOUND_MARGIN
    ex = extra(tq)

    def bounded(*a):
        return _flash_call(kernel_fn, n_streams, a[0], a[1], *a[2:5], a[5:], name=name, **lat)

    def online(*a):
        return _flash_call(kernel_fn, n_streams, a[0], None, *a[2:5], a[5:],
                           name=name + "_online", **lat)

    y = lax.cond(jnp.max(bound) <= SCORE_BOUND_LIMIT, bounded, online,
                 qt, bound[:, None, :], k, vt, gate, *ex)
    if not update_ctx:
        return y, None
    y_ctx = _flash_call(kernel_fn, n_streams, qt, None, k, vt, gate, extra(n_ctx),
                        q_tiles=1, q_tile0=n_lat // n_ctx, tq=n_ctx,
                        k_len=n_ctx, k_blk0=n_lat // n_ctx, tk=n_ctx,
                        name=name + "_ctx", **common)
    return y, y_ctx


def _post_kernel(mod_row, n_y, *refs):
    y_refs = refs[:n_y]
    wo_ref, x_ref, mod_ref, o_ref = refs[n_y:]
    d = x_ref.shape[-1]
    acc = None
    off = 0
    for y_ref in y_refs:
        w = y_ref.shape[0]
        part = _dot(wo_ref[:, off:off + w], y_ref[...])
        acc = part if acc is None else acc + part
        off += w
    gate = mod_ref[mod_row:mod_row + 1, 2 * d:3 * d]
    o_ref[...] = x_ref[...] + gate * jnp.transpose(acc)


def _post(ys, w_out_t, x, mod, mod_row, name):
    n_rows, d = x.shape
    t = POST_TILE if n_rows % POST_TILE == 0 else ROW_TILE
    width = w_out_t.shape[1]
    in_specs = [pl.BlockSpec((y.shape[0], t), lambda i: (0, i)) for y in ys] + [
        pl.BlockSpec((d, width), lambda i: (0, 0)),
        pl.BlockSpec((t, d), lambda i: (i, 0)),
        pl.BlockSpec((8, 3 * d), lambda i: (0, 0)),
    ]
    return pl.pallas_call(
        functools.partial(_post_kernel, mod_row, len(ys)),
        grid=(n_rows // t,),
        in_specs=in_specs,
        out_specs=pl.BlockSpec((t, d), lambda i: (i, 0)),
        out_shape=jax.ShapeDtypeStruct((n_rows, d), F32),
        compiler_params=_compiler_params(1),
        name=name,
    )(*ys, w_out_t, x, mod)


def _rope_tables(n_lat, n_ctx, dim):
    tok = np.arange(n_lat)
    rows, cols = tok // GRID_W, tok % GRID_W
    n_freq = dim // 4
    inv = ROPE_THETA ** (-np.arange(n_freq) / n_freq)
    ang = np.concatenate([rows[:, None] * inv, cols[:, None] * inv], axis=-1)
    ang = np.concatenate([ang, np.zeros((n_ctx, dim // 2))], axis=0)
    return (jnp.asarray(np.cos(ang).T, dtype=F32), jnp.asarray(np.sin(ang).T, dtype=F32))


def kernel(x, c, ctx, c_ctx, e_norm_g, e_w_mod, e_b_mod, e_w_in, a_g_cq, a_w_uq, a_g_ckv, a_w_ukv, a_g_qn, a_g_kn, b_g_qn, b_g_kn, b_lam_q1, b_lam_k1, b_lam_q2, b_lam_k2, b_g_out, e_w_out, o_norm_g, o_w_mod, o_b_mod, o_w_in, c_g_qn, c_g_kn, o_w_out):
    batch, n_lat, d = x.shape
    n_ctx = ctx.shape[1]
    assert batch == 1 and n_ctx == ROW_TILE and n_lat % ROW_TILE == 0
    depth = e_norm_g.shape[0] + o_norm_g.shape[0]

    cvec = jnp.zeros((8, d), F32).at[0].set(c[0]).at[1].set(c_ctx)
    rope_a = _rope_tables(n_lat, n_ctx, A_ROPE)
    rope_b = _rope_tables(n_lat, n_ctx, B_D)
    rope_c = _rope_tables(n_lat, n_ctx, C_HD)

    x_lat, x_ctx = x[0], ctx[0]
    for layer in range(depth):
        i = layer // 2
        update_ctx = layer < depth - 1
        pf = dict(n_lat=n_lat, n_ctx=n_ctx, update_ctx=update_ctx)
        if layer % 2 == 0:
            lam_init = 0.8 - 0.6 * math.exp(-0.3 * layer)
            mod = _modulation(cvec, e_w_mod[i], e_b_mod[i])
            qta, ka, vta, gta, qna, kna, qtb, kb, vtb, gtb, qnb, knb = _even_pre(
                x_lat, x_ctx, mod, e_norm_g[i], e_w_in[i], a_g_cq[i], a_w_uq[i],
                a_g_ckv[i], a_w_ukv[i], a_g_qn[i], a_g_kn[i], b_g_qn[i], b_g_kn[i],
                rope_a, rope_b)
            ya = _prefix_flash(_flash_gate_kernel, 1, qta, ka, vta, gta, qna, kna,
                               lambda tq: [], n_heads=A_HEADS, kv_group=1,
                               name="flash_mla", **pf)
            lam_vecs = jnp.stack([b_lam_q1[i], b_lam_k1[i], b_lam_q2[i], b_lam_k2[i]]).astype(F32)
            yb = _prefix_flash(
                functools.partial(_flash_diff_kernel, lam_init), 2, qtb, kb, vtb, gtb, qnb, knb,
                lambda tq: [lam_vecs, _bcast_rows(b_g_out[i], tq)],
                n_heads=B_HEADS, kv_group=1, name="flash_diff", **pf)
            ys, w_out = [ya, yb], e_w_out[i]
        else:
            mod = _modulation(cvec, o_w_mod[i], o_b_mod[i])
            qtc, kc, vtc, gtc, qnc, knc = _odd_pre(
                x_lat, x_ctx, mod, o_norm_g[i], o_w_in[i], c_g_qn[i], c_g_kn[i], rope_c)
            ys = [_prefix_flash(_flash_gate_kernel, 1, qtc, kc, vtc, gtc, qnc, knc,
                                lambda tq: [], n_heads=C_HEADS,
                                kv_group=C_HEADS // C_KV_HEADS, name="flash_gqa", **pf)]
            w_out = o_w_out[i]
        w_out_t = w_out.T.astype(BF16)
        if update_ctx:
            x_ctx = _post([y[1] for y in ys], w_out_t, x_ctx, mod, 1, "post_ctx")
        x_lat = _post([y[0] for y in ys], w_out_t, x_lat, mod, 0, "post")
    return x_lat.reshape(batch, n_lat, d)
```

```python
import functools
import math

import jax
import jax.numpy as jnp
import numpy as np
from jax import lax
from jax.experimental import pallas as pl
from jax.experimental.pallas import tpu as pltpu

F32 = jnp.float32
BF16 = jnp.bfloat16

GRID_W = 64
ROPE_THETA = 10000.0
EPS = 1e-6
LOG2E = math.log2(math.e)

A_HEADS, A_Q_LORA, A_KV_LORA, A_NOPE, A_ROPE, A_V = 8, 256, 128, 64, 32, 64
A_QK = A_NOPE + A_ROPE
B_HEADS, B_D = 4, 64
B_V = 2 * B_D
C_HEADS, C_KV_HEADS, C_HD = 8, 2, 128

HEAD_PAD = 128
ROW_TILE = 256
POST_TILE = 1024
FLASH_QUERY_LANES = 1024
BOUNDED_KEY_CHUNK = 3328
ONLINE_KEY_CHUNK = 1280
CHUNKS_PER_TRIP = 4
NEG_BIG = -1e30
SCORE_BOUND_LIMIT = 48.0
BOUND_MARGIN = 1.0 + 2.0 ** -6
VMEM_LIMIT_BYTES = 56 * 1024 * 1024


def _silu(v):
    return v / (1.0 + jnp.exp(-v))


def _dot(a, b):
    return jnp.dot(a, b, preferred_element_type=F32)


def _dot_nt(a, b):
    return lax.dot_general(a, b, (((1,), (1,)), ((), ())), preferred_element_type=F32)


def _rms_rows(v, g):
    ms = jnp.mean(v * v, axis=0, keepdims=True)
    return v * lax.rsqrt(ms + EPS) * g


def _round_with_sqnorm(v):
    return v.astype(BF16), jnp.sum(v * v, axis=0, keepdims=True)


def _rope_rows(v, cos, sin):
    half = v.shape[0] // 2
    x1, x2 = v[:half], v[half:]
    return jnp.concatenate([x1 * cos - x2 * sin, x2 * cos + x1 * sin], axis=0)


def _compiler_params(n_grid):
    return pltpu.CompilerParams(
        dimension_semantics=("arbitrary",) * n_grid,
        vmem_limit_bytes=VMEM_LIMIT_BYTES,
    )


def _mod_kernel(c_ref, w_ref, b_ref, o_ref):
    a = _silu(c_ref[...])
    w = w_ref[...]
    a_hi = a.astype(BF16)
    a_lo = (a - a_hi.astype(F32)).astype(BF16)
    w_hi = w.astype(BF16)
    w_lo = (w - w_hi.astype(F32)).astype(BF16)
    acc = _dot(a_hi, w_hi) + _dot(a_hi, w_lo) + _dot(a_lo, w_hi)
    o_ref[...] = acc + b_ref[...]


def _modulation(cvec, w_mod, b_mod):
    d, d3 = w_mod.shape
    col = 768
    return pl.pallas_call(
        _mod_kernel,
        grid=(d3 // col,),
        in_specs=[
            pl.BlockSpec((8, d), lambda j: (0, 0)),
            pl.BlockSpec((d, col), lambda j: (0, j)),
            pl.BlockSpec((1, col), lambda j: (0, j)),
        ],
        out_specs=pl.BlockSpec((8, col), lambda j: (0, j)),
        out_shape=jax.ShapeDtypeStruct((8, d3), F32),
        compiler_params=_compiler_params(1),
        name="modulation",
    )(cvec, w_mod, b_mod.reshape(1, d3))


def _token_specs(t, d, n_lat_tiles):
    return [pl.BlockSpec((t, d), lambda i: (jnp.minimum(i, n_lat_tiles - 1), 0)),
            pl.BlockSpec((t, d), lambda i: (0, 0))]


def _modulated_rows(x_ref, xc_ref, mod_ref, g_ref, is_ctx):
    d = x_ref.shape[-1]
    x = jnp.where(is_ctx, xc_ref[...], x_ref[...])
    sh = jnp.where(is_ctx, mod_ref[1:2, 0:d], mod_ref[0:1, 0:d])
    sc = jnp.where(is_ctx, mod_ref[1:2, d:2 * d], mod_ref[0:1, d:2 * d])
    ms = jnp.mean(x * x, axis=-1, keepdims=True)
    y = x * lax.rsqrt(ms + EPS) * g_ref[...]
    return (y * (1.0 + sc) + sh).astype(BF16)


def _even_pre_kernel(n_lat_tiles,
                     x_ref, xc_ref, mod_ref, g_ref, wt_ref,
                     gcq_ref, wuq_ref, gckv_ref, wukv_ref, gqa_ref, gka_ref,
                     cosa_ref, sina_ref, gqb_ref, gkb_ref, cosb_ref, sinb_ref,
                     qta_ref, ka_ref, vta_ref, gta_ref, qna_ref, kna_ref,
                     qtb_ref, kb_ref, vtb_ref, gtb_ref, qnb_ref, knb_ref):
    is_ctx = pl.program_id(0) >= n_lat_tiles
    hb = _modulated_rows(x_ref, xc_ref, mod_ref, g_ref, is_ctx)
    tok = hb.shape[0]

    o_cq, o_ckv, o_kr = 0, A_Q_LORA, A_Q_LORA + A_KV_LORA
    o_ga = o_kr + A_ROPE
    o_bq = o_ga + A_HEADS * A_V
    o_bk = o_bq + 2 * B_HEADS * B_D
    o_bv = o_bk + 2 * B_HEADS * B_D
    o_gb = o_bv + B_HEADS * B_V
    o_end = o_gb + B_HEADS * B_V

    def proj(lo, hi):
        return _dot_nt(wt_ref[lo:hi, :], hb)

    cq = _rms_rows(proj(o_cq, o_ckv), gcq_ref[...]).astype(BF16)
    q_all = _dot(wuq_ref[...], cq)
    cos_a, sin_a = cosa_ref[...], sina_ref[...]
    q_scale = (A_QK ** -0.5) * LOG2E
    pad_a = jnp.zeros((HEAD_PAD - A_QK, tok), F32)
    norms = []
    for h in range(A_HEADS):
        qh = _rms_rows(q_all[h * A_QK:(h + 1) * A_QK], gqa_ref[...])
        qh = jnp.concatenate(
            [qh[:A_NOPE], _rope_rows(qh[A_NOPE:], cos_a, sin_a), pad_a], axis=0)
        qta_ref[h], n2 = _round_with_sqnorm(qh * q_scale)
        norms.append(n2)
    qna_ref[...] = jnp.concatenate(norms, axis=0)

    ckv_kr = proj(o_ckv, o_ga)
    ckv = _rms_rows(ckv_kr[:A_KV_LORA], gckv_ref[...]).astype(BF16)
    kr = ckv_kr[A_KV_LORA:]
    kv_all = _dot(wukv_ref[...], ckv)
    norms = []
    for h in range(A_HEADS):
        base = h * (A_NOPE + A_V)
        kh = jnp.concatenate([kv_all[base:base + A_NOPE], kr], axis=0)
        kh = _rms_rows(kh, gka_ref[...])
        kh = jnp.concatenate(
            [kh[:A_NOPE], _rope_rows(kh[A_NOPE:], cos_a, sin_a), pad_a], axis=0)
        norms.append(_round_with_sqnorm(kh)[1])
        ka_ref[h] = jnp.transpose(kh).astype(BF16)
        vh = kv_all[base + A_NOPE:base + A_NOPE + A_V]
        vta_ref[h] = jnp.concatenate([vh, jnp.zeros_like(vh)], axis=0).astype(BF16)
    kna_ref[...] = jnp.concatenate(norms, axis=0)

    gta_ref[...] = _silu(proj(o_ga, o_bq))

    cos_b, sin_b = cosb_ref[...], sinb_ref[...]
    qb_scale = (B_D ** -0.5) * LOG2E
    pad_b = jnp.zeros((B_D, tok), F32)
    bq = proj(o_bq, o_bk)
    bk = proj(o_bk, o_bv)
    q_norms, k_norms = [], []
    for h in range(B_HEADS):
        ks = []
        for j in range(2):
            m = 2 * h + j
            qm = _rms_rows(bq[m * B_D:(m + 1) * B_D], gqb_ref[...])
            qm = _rope_rows(qm, cos_b, sin_b) * qb_scale
            parts = [qm, pad_b] if j == 0 else [pad_b, qm]
            qtb_ref[m], n2 = _round_with_sqnorm(jnp.concatenate(parts, axis=0))
            q_norms.append(n2)
            km = _rms_rows(bk[m * B_D:(m + 1) * B_D], gkb_ref[...])
            km = _rope_rows(km, cos_b, sin_b)
            k_norms.append(_round_with_sqnorm(km)[1])
            ks.append(km)
        kb_ref[h] = jnp.transpose(jnp.concatenate(ks, axis=0)).astype(BF16)
    qnb_ref[...] = jnp.concatenate(q_norms, axis=0)
    knb_ref[...] = jnp.concatenate(k_norms, axis=0)
    bv = proj(o_bv, o_gb)
    for h in range(B_HEADS):
        vtb_ref[h] = bv[h * B_V:(h + 1) * B_V].astype(BF16)
    gtb_ref[...] = _silu(proj(o_gb, o_end))


def _bcast_rows(g, tok):
    return jnp.broadcast_to(g.astype(F32)[:, None], (g.shape[0], tok))


def _even_pre(x_lat, x_ctx, mod, norm_g, w_in, a_g_cq, a_w_uq, a_g_ckv, a_w_ukv,
              a_g_qn, a_g_kn, b_g_qn, b_g_kn, rope_a, rope_b):
    n_lat, d = x_lat.shape
    t = ROW_TILE
    seq = n_lat + x_ctx.shape[0]
    n_tiles = seq // t
    e_in = w_in.shape[1]
    const2 = lambda i: (0, 0)
    tok2 = lambda i: (0, i)
    tok3 = lambda i: (0, 0, i)
    row3 = lambda i: (0, i, 0)
    in_specs = _token_specs(t, d, n_lat // t) + [
        pl.BlockSpec((8, 3 * d), const2),
        pl.BlockSpec((1, d), const2),
        pl.BlockSpec((e_in, d), const2),
        pl.BlockSpec((A_Q_LORA, t), const2),
        pl.BlockSpec((A_HEADS * A_QK, A_Q_LORA), const2),
        pl.BlockSpec((A_KV_LORA, t), const2),
        pl.BlockSpec((A_HEADS * (A_NOPE + A_V), A_KV_LORA), const2),
        pl.BlockSpec((A_QK, t), const2),
        pl.BlockSpec((A_QK, t), const2),
        pl.BlockSpec((A_ROPE // 2, t), tok2),
        pl.BlockSpec((A_ROPE // 2, t), tok2),
        pl.BlockSpec((B_D, t), const2),
        pl.BlockSpec((B_D, t), const2),
        pl.BlockSpec((B_D // 2, t), tok2),
        pl.BlockSpec((B_D // 2, t), tok2),
    ]
    out_shape = [
        jax.ShapeDtypeStruct((A_HEADS, HEAD_PAD, seq), BF16),
        jax.ShapeDtypeStruct((A_HEADS, seq, HEAD_PAD), BF16),
        jax.ShapeDtypeStruct((A_HEADS, 2 * A_V, seq), BF16),
        jax.ShapeDtypeStruct((A_HEADS * A_V, seq), F32),
        jax.ShapeDtypeStruct((A_HEADS, seq), F32),
        jax.ShapeDtypeStruct((A_HEADS, seq), F32),
        jax.ShapeDtypeStruct((2 * B_HEADS, HEAD_PAD, seq), BF16),
        jax.ShapeDtypeStruct((B_HEADS, seq, HEAD_PAD), BF16),
        jax.ShapeDtypeStruct((B_HEADS, B_V, seq), BF16),
        jax.ShapeDtypeStruct((B_HEADS * B_V, seq), F32),
        jax.ShapeDtypeStruct((2 * B_HEADS, seq), F32),
        jax.ShapeDtypeStruct((2 * B_HEADS, seq), F32),
    ]
    out_specs = [
        pl.BlockSpec((A_HEADS, HEAD_PAD, t), tok3),
        pl.BlockSpec((A_HEADS, t, HEAD_PAD), row3),
        pl.BlockSpec((A_HEADS, 2 * A_V, t), tok3),
        pl.BlockSpec((A_HEADS * A_V, t), tok2),
        pl.BlockSpec((A_HEADS, t), tok2),
        pl.BlockSpec((A_HEADS, t), tok2),
        pl.BlockSpec((2 * B_HEADS, HEAD_PAD, t), tok3),
        pl.BlockSpec((B_HEADS, t, HEAD_PAD), row3),
        pl.BlockSpec((B_HEADS, B_V, t), tok3),
        pl.BlockSpec((B_HEADS * B_V, t), tok2),
        pl.BlockSpec((2 * B_HEADS, t), tok2),
        pl.BlockSpec((2 * B_HEADS, t), tok2),
    ]
    return pl.pallas_call(
        functools.partial(_even_pre_kernel, n_lat // t),
        grid=(n_tiles,),
        in_specs=in_specs,
        out_specs=out_specs,
        out_shape=out_shape,
        compiler_params=_compiler_params(1),
        name="even_pre",
    )(x_lat, x_ctx, mod, norm_g.reshape(1, d), w_in.T.astype(BF16),
      _bcast_rows(a_g_cq, t), a_w_uq.T.astype(BF16),
      _bcast_rows(a_g_ckv, t), a_w_ukv.T.astype(BF16),
      _bcast_rows(a_g_qn, t), _bcast_rows(a_g_kn, t), rope_a[0], rope_a[1],
      _bcast_rows(b_g_qn, t), _bcast_rows(b_g_kn, t), rope_b[0], rope_b[1])


def _odd_pre_kernel(n_lat_tiles,
                    x_ref, xc_ref, mod_ref, g_ref, wt_ref, gq_ref, gk_ref, cos_ref, sin_ref,
                    qt_ref, k_ref, vt_ref, gt_ref, qn_ref, kn_ref):
    is_ctx = pl.program_id(0) >= n_lat_tiles
    hb = _modulated_rows(x_ref, xc_ref, mod_ref, g_ref, is_ctx)
    cos, sin = cos_ref[...], sin_ref[...]
    o_k = C_HEADS * C_HD
    o_v = o_k + C_KV_HEADS * C_HD
    o_g = o_v + C_KV_HEADS * C_HD
    o_end = o_g + C_HEADS * C_HD
    q_scale = (C_HD ** -0.5) * LOG2E
    q = _dot_nt(wt_ref[0:o_k, :], hb)
    norms = []
    for h in range(C_HEADS):
        qh = _rms_rows(q[h * C_HD:(h + 1) * C_HD], gq_ref[...])
        qt_ref[h], n2 = _round_with_sqnorm(_rope_rows(qh, cos, sin) * q_scale)
        norms.append(n2)
    qn_ref[...] = jnp.concatenate(norms, axis=0)
    kv = _dot_nt(wt_ref[o_k:o_g, :], hb)
    norms = []
    for h in range(C_KV_HEADS):
        kh = _rms_rows(kv[h * C_HD:(h + 1) * C_HD], gk_ref[...])
        kh = _rope_rows(kh, cos, sin)
        norms.append(_round_with_sqnorm(kh)[1])
        k_ref[h] = jnp.transpose(kh).astype(BF16)
        vt_ref[h] = kv[(C_KV_HEADS + h) * C_HD:(C_KV_HEADS + h + 1) * C_HD].astype(BF16)
    kn_ref[...] = jnp.concatenate(norms, axis=0)
    gt_ref[...] = _silu(_dot_nt(wt_ref[o_g:o_end, :], hb))


def _odd_pre(x_lat, x_ctx, mod, norm_g, w_in, g_qn, g_kn, rope_c):
    n_lat, d = x_lat.shape
    t = ROW_TILE
    seq = n_lat + x_ctx.shape[0]
    o_in = w_in.shape[1]
    const2 = lambda i: (0, 0)
    tok2 = lambda i: (0, i)
    tok3 = lambda i: (0, 0, i)
    in_specs = _token_specs(t, d, n_lat // t) + [
        pl.BlockSpec((8, 3 * d), const2),
        pl.BlockSpec((1, d), const2),
        pl.BlockSpec((o_in, d), const2),
        pl.BlockSpec((C_HD, t), const2),
        pl.BlockSpec((C_HD, t), const2),
        pl.BlockSpec((C_HD // 2, t), tok2),
        pl.BlockSpec((C_HD // 2, t), tok2),
    ]
    out_shape = [
        jax.ShapeDtypeStruct((C_HEADS, C_HD, seq), BF16),
        jax.ShapeDtypeStruct((C_KV_HEADS, seq, C_HD), BF16),
        jax.ShapeDtypeStruct((C_KV_HEADS, C_HD, seq), BF16),
        jax.ShapeDtypeStruct((C_HEADS * C_HD, seq), F32),
        jax.ShapeDtypeStruct((C_HEADS, seq), F32),
        jax.ShapeDtypeStruct((C_KV_HEADS, seq), F32),
    ]
    out_specs = [
        pl.BlockSpec((C_HEADS, C_HD, t), tok3),
        pl.BlockSpec((C_KV_HEADS, t, C_HD), lambda i: (0, i, 0)),
        pl.BlockSpec((C_KV_HEADS, C_HD, t), tok3),
        pl.BlockSpec((C_HEADS * C_HD, t), tok2),
        pl.BlockSpec((C_HEADS, t), tok2),
        pl.BlockSpec((C_KV_HEADS, t), tok2),
    ]
    return pl.pallas_call(
        functools.partial(_odd_pre_kernel, n_lat // t),
        grid=(seq // t,),
        in_specs=in_specs,
        out_specs=out_specs,
        out_shape=out_shape,
        compiler_params=_compiler_params(1),
        name="odd_pre",
    )(x_lat, x_ctx, mod, norm_g.reshape(1, d), w_in.T.astype(BF16),
      _bcast_rows(g_qn, t), _bcast_rows(g_kn, t), rope_c[0], rope_c[1])


def _flash_streams(qts, k_ref, vt_ref, s_ref, acc_ref, n_chunks, tk):
    n_s = len(qts)
    tq = qts[0].shape[1]
    for st in range(n_s):
        acc_ref[st] = jnp.zeros(acc_ref.shape[1:], F32)

    def scores(slot, c):
        start = pl.multiple_of(c * tk, tk)
        k = k_ref[0, pl.ds(start, tk), :]
        maxes = []
        for st in range(n_s):
            s = _dot(k, qts[st])
            s_ref[st, slot] = s
            maxes.append(jnp.max(s, axis=0, keepdims=True))
        return tuple(maxes)

    def consume(slot, c, state, maxes):
        ms, ls = state
        start = pl.multiple_of(c * tk, tk)
        vt = vt_ref[0, :, pl.ds(start, tk)]
        new_ms, new_ls = [], []
        for st in range(n_s):
            m_new = jnp.maximum(ms[st], maxes[st])
            alpha = jnp.exp2(ms[st] - m_new)
            p = jnp.exp2(s_ref[st, slot] - m_new)
            new_ls.append(alpha * ls[st] + jnp.sum(p, axis=0, keepdims=True))
            acc_ref[st] = alpha * acc_ref[st] + _dot(vt, p.astype(BF16))
            new_ms.append(m_new)
        return tuple(new_ms), tuple(new_ls)

    def step(c, slot, state, mx_cur):
        mx_next = scores(1 - slot, c + 1)
        return consume(slot, c, state, mx_cur), mx_next

    def group(j, carry):
        state, mx = carry
        for u in range(CHUNKS_PER_TRIP):
            state, mx = step(j * CHUNKS_PER_TRIP + u, u % 2, state, mx)
        return state, mx

    state = (tuple(jnp.full((1, tq), NEG_BIG, F32) for _ in range(n_s)),
             tuple(jnp.zeros((1, tq), F32) for _ in range(n_s)))
    mx = scores(0, 0)
    n_trips = (n_chunks - 1) // CHUNKS_PER_TRIP
    state, mx = lax.fori_loop(0, n_trips, group, (state, mx))
    for c in range(n_trips * CHUNKS_PER_TRIP, n_chunks - 1):
        state, mx = step(c, c % 2, state, mx)
    state = consume((n_chunks - 1) % 2, n_chunks - 1, state, mx)
    _, ls = state
    return [acc_ref[st] / ls[st] for st in range(n_s)]


def _bounded_streams(qts, bounds, k_ref, vt_ref, acc_ref, n_chunks, tk):
    n_s = len(qts)
    tq = qts[0].shape[1]
    q_all = jnp.concatenate(qts, axis=1) if n_s > 1 else qts[0]
    b_all = jnp.concatenate(bounds, axis=1) if n_s > 1 else bounds[0]
    l_all = None
    for c in range(n_chunks):
        k = k_ref[0, c * tk:(c + 1) * tk, :]
        vt = vt_ref[0, :, c * tk:(c + 1) * tk]
        p = jnp.exp2(_dot(k, q_all) - b_all)
        p_sum = jnp.sum(p, axis=0, keepdims=True)
        pv = _dot(vt, p.astype(BF16))
        if c == 0:
            l_all = p_sum
            acc_ref[...] = pv
        else:
            l_all = l_all + p_sum
            acc_ref[...] += pv
    o_all = acc_ref[...] / l_all
    return [o_all[:, st * tq:(st + 1) * tq] for st in range(n_s)]


def _attend(bounded, n_s, n_chunks, tk, refs):
    qts = [r[0] for r in refs[:n_s]]
    pos = n_s
    if bounded:
        bounds = [r[0] for r in refs[pos:pos + n_s]]
        pos += n_s
    k_ref, vt_ref = refs[pos:pos + 2]
    pos += 2
    if bounded:
        yt_ref, acc_ref = refs[-2:]
        outs = _bounded_streams(qts, bounds, k_ref, vt_ref, acc_ref, n_chunks, tk)
        return outs, refs[pos:-2], yt_ref
    yt_ref, s_ref, acc_ref = refs[-3:]
    outs = _flash_streams(qts, k_ref, vt_ref, s_ref, acc_ref, n_chunks, tk)
    return outs, refs[pos:-3], yt_ref


def _flash_gate_kernel(bounded, n_chunks, tk, *refs):
    (o,), (gate_ref,), yt_ref = _attend(bounded, 1, n_chunks, tk, refs)
    yt_ref[...] = (o[:gate_ref.shape[0]] * gate_ref[...]).astype(BF16)


def _flash_diff_kernel(lam_init, bounded, n_chunks, tk, *refs):
    (o1, o2), (gate_ref, lam_ref, gout_ref), yt_ref = _attend(bounded, 2, n_chunks, tk, refs)
    lv = lam_ref[...]
    lam = (jnp.exp(jnp.sum(lv[0:1] * lv[1:2], axis=-1, keepdims=True))
           - jnp.exp(jnp.sum(lv[2:3] * lv[3:4], axis=-1, keepdims=True)) + lam_init)
    o = o1 - lam * o2
    o = _rms_rows(o, gout_ref[...]) * (1.0 - lam_init)
    yt_ref[...] = (o * gate_ref[...]).astype(BF16)


def _flash_call(kernel_fn, n_streams, qt, bound, k, vt, gate, extra, *, n_heads, kv_group,
                q_tiles, q_tile0, tq, k_len, k_blk0, tk, name):
    dv = vt.shape[1]
    dv_out = gate.shape[0] // n_heads
    n_chunks = k_len // tk
    bounded = bound is not None

    def stream_specs(rows):
        return [
            pl.BlockSpec((1, rows, tq),
                         functools.partial(lambda s, h, i: (n_streams * h + s, 0, q_tile0 + i), s))
            for s in range(n_streams)
        ]

    in_specs = stream_specs(HEAD_PAD) + (stream_specs(1) if bounded else []) + [
        pl.BlockSpec((1, k_len, HEAD_PAD), lambda h, i: (h // kv_group, k_blk0, 0)),
        pl.BlockSpec((1, dv, k_len), lambda h, i: (h // kv_group, 0, k_blk0)),
        pl.BlockSpec((dv_out, tq), lambda h, i: (h, q_tile0 + i)),
    ] + [pl.BlockSpec(e.shape, lambda h, i: (0, 0)) for e in extra]
    if bounded:
        scratch = [pltpu.VMEM((dv, n_streams * tq), F32)]
    else:
        scratch = [pltpu.VMEM((n_streams, 2, tk, tq), F32),
                   pltpu.VMEM((n_streams, dv, tq), F32)]
    args = [qt] * n_streams + ([bound] * n_streams if bounded else [])
    return pl.pallas_call(
        functools.partial(kernel_fn, bounded, n_chunks, tk),
        grid=(n_heads, q_tiles),
        in_specs=in_specs,
        out_specs=pl.BlockSpec((dv_out, tq), lambda h, i: (h, i)),
        out_shape=jax.ShapeDtypeStruct((n_heads * dv_out, q_tiles * tq), BF16),
        scratch_shapes=scratch,
        compiler_params=_compiler_params(2),
        name=name,
    )(*args, k, vt, gate, *extra)


def _prefix_flash(kernel_fn, n_streams, qt, k, vt, gate, qn2, kn2, extra, *, n_heads,
                  kv_group, n_lat, n_ctx, update_ctx, name):
    seq = n_lat + n_ctx
    tq = FLASH_QUERY_LANES // n_streams
    if n_lat % tq != 0:
        tq = ROW_TILE
    common = dict(n_heads=n_heads, kv_group=kv_group)
    lat = dict(q_tiles=n_lat // tq, q_tile0=0, tq=tq, k_len=seq, k_blk0=0, **common)

    def key_chunk(size):
        return size if seq % size == 0 else ROW_TILE

    k_max2 = jnp.repeat(jnp.max(kn2, axis=1), qn2.shape[0] // kn2.shape[0])
    bound = jnp.sqrt(qn2[:, :n_lat] * k_max2[:, None]) * BOUND_MARGIN
    ex = extra(tq)

    def bounded(*a):
        return _flash_call(kernel_fn, n_streams, a[0], a[1], *a[2:5], a[5:], name=name,
                           tk=key_chunk(BOUNDED_KEY_CHUNK), **lat)

    def online(*a):
        return _flash_call(kernel_fn, n_streams, a[0], None, *a[2:5], a[5:],
                           name=name + "_online", tk=key_chunk(ONLINE_KEY_CHUNK), **lat)

    y = lax.cond(jnp.max(bound) <= SCORE_BOUND_LIMIT, bounded, online,
                 qt, bound[:, None, :], k, vt, gate, *ex)
    if not update_ctx:
        return y, None
    y_ctx = _flash_call(kernel_fn, n_streams, qt, None, k, vt, gate, extra(n_ctx),
                        q_tiles=1, q_tile0=n_lat // n_ctx, tq=n_ctx,
                        k_len=n_ctx, k_blk0=n_lat // n_ctx, tk=n_ctx,
                        name=name + "_ctx", **common)
    return y, y_ctx


def _post_kernel(mod_row, n_y, *refs):
    y_refs = refs[:n_y]
    wo_ref, x_ref, mod_ref, o_ref = refs[n_y:]
    d = x_ref.shape[-1]
    acc = None
    off = 0
    for y_ref in y_refs:
        w = y_ref.shape[0]
        part = _dot(wo_ref[:, off:off + w], y_ref[...])
        acc = part if acc is None else acc + part
        off += w
    gate = mod_ref[mod_row:mod_row + 1, 2 * d:3 * d]
    o_ref[...] = x_ref[...] + gate * jnp.transpose(acc)


def _post(ys, w_out_t, x, mod, mod_row, name):
    n_rows, d = x.shape
    t = POST_TILE if n_rows % POST_TILE == 0 else ROW_TILE
    width = w_out_t.shape[1]
    in_specs = [pl.BlockSpec((y.shape[0], t), lambda i: (0, i)) for y in ys] + [
        pl.BlockSpec((d, width), lambda i: (0, 0)),
        pl.BlockSpec((t, d), lambda i: (i, 0)),
        pl.BlockSpec((8, 3 * d), lambda i: (0, 0)),
    ]
    return pl.pallas_call(
        functools.partial(_post_kernel, mod_row, len(ys)),
        grid=(n_rows // t,),
        in_specs=in_specs,
        out_specs=pl.BlockSpec((t, d), lambda i: (i, 0)),
        out_shape=jax.ShapeDtypeStruct((n_rows, d), F32),
        compiler_params=_compiler_params(1),
        name=name,
    )(*ys, w_out_t, x, mod)


def _rope_tables(n_lat, n_ctx, dim):
    tok = np.arange(n_lat)
    rows, cols = tok // GRID_W, tok % GRID_W
    n_freq = dim // 4
    inv = ROPE_THETA ** (-np.arange(n_freq) / n_freq)
    ang = np.concatenate([rows[:, None] * inv, cols[:, None] * inv], axis=-1)
    ang = np.concatenate([ang, np.zeros((n_ctx, dim // 2))], axis=0)
    return (jnp.asarray(np.cos(ang).T, dtype=F32), jnp.asarray(np.sin(ang).T, dtype=F32))


def kernel(x, c, ctx, c_ctx, e_norm_g, e_w_mod, e_b_mod, e_w_in, a_g_cq, a_w_uq, a_g_ckv, a_w_ukv, a_g_qn, a_g_kn, b_g_qn, b_g_kn, b_lam_q1, b_lam_k1, b_lam_q2, b_lam_k2, b_g_out, e_w_out, o_norm_g, o_w_mod, o_b_mod, o_w_in, c_g_qn, c_g_kn, o_w_out):
    batch, n_lat, d = x.shape
    n_ctx = ctx.shape[1]
    assert batch == 1 and n_ctx == ROW_TILE and n_lat % ROW_TILE == 0
    depth = e_norm_g.shape[0] + o_norm_g.shape[0]

    cvec = jnp.zeros((8, d), F32).at[0].set(c[0]).at[1].set(c_ctx)
    rope_a = _rope_tables(n_lat, n_ctx, A_ROPE)
    rope_b = _rope_tables(n_lat, n_ctx, B_D)
    rope_c = _rope_tables(n_lat, n_ctx, C_HD)

    x_lat, x_ctx = x[0], ctx[0]
    for layer in range(depth):
        i = layer // 2
        update_ctx = layer < depth - 1
        pf = dict(n_lat=n_lat, n_ctx=n_ctx, update_ctx=update_ctx)
        if layer % 2 == 0:
            lam_init = 0.8 - 0.6 * math.exp(-0.3 * layer)
            mod = _modulation(cvec, e_w_mod[i], e_b_mod[i])
            qta, ka, vta, gta, qna, kna, qtb, kb, vtb, gtb, qnb, knb = _even_pre(
                x_lat, x_ctx, mod, e_norm_g[i], e_w_in[i], a_g_cq[i], a_w_uq[i],
                a_g_ckv[i], a_w_ukv[i], a_g_qn[i], a_g_kn[i], b_g_qn[i], b_g_kn[i],
                rope_a, rope_b)
            ya = _prefix_flash(_flash_gate_kernel, 1, qta, ka, vta, gta, qna, kna,
                               lambda tq: [], n_heads=A_HEADS, kv_group=1,
                               name="flash_mla", **pf)
            lam_vecs = jnp.stack([b_lam_q1[i], b_lam_k1[i], b_lam_q2[i], b_lam_k2[i]]).astype(F32)
            yb = _prefix_flash(
                functools.partial(_flash_diff_kernel, lam_init), 2, qtb, kb, vtb, gtb, qnb, knb,
                lambda tq: [lam_vecs, _bcast_rows(b_g_out[i], tq)],
                n_heads=B_HEADS, kv_group=1, name="flash_diff", **pf)
            ys, w_out = [ya, yb], e_w_out[i]
        else:
            mod = _modulation(cvec, o_w_mod[i], o_b_mod[i])
            qtc, kc, vtc, gtc, qnc, knc = _odd_pre(
                x_lat, x_ctx, mod, o_norm_g[i], o_w_in[i], c_g_qn[i], c_g_kn[i], rope_c)
            ys = [_prefix_flash(_flash_gate_kernel, 1, qtc, kc, vtc, gtc, qnc, knc,
                                lambda tq: [], n_heads=C_HEADS,
                                kv_group=C_HEADS // C_KV_HEADS, name="flash_gqa", **pf)]
            w_out = o_w_out[i]
        w_out_t = w_out.T.astype(BF16)
        if update_ctx:
            x_ctx = _post([y[1] for y in ys], w_out_t, x_ctx, mod, 1, "post_ctx")
        x_lat = _post([y[0] for y in ys], w_out_t, x_lat, mod, 0, "post")
    return x_lat.reshape(batch, n_lat, d)
```

```python
import functools
import math

import jax
import jax.numpy as jnp
import numpy as np
from jax import lax
from jax.experimental import pallas as pl
from jax.experimental.pallas import tpu as pltpu

F32 = jnp.float32
BF16 = jnp.bfloat16

GRID_W = 64
ROPE_THETA = 10000.0
EPS = 1e-6
LOG2E = math.log2(math.e)

A_HEADS, A_Q_LORA, A_KV_LORA, A_NOPE, A_ROPE, A_V = 8, 256, 128, 64, 32, 64
A_QK = A_NOPE + A_ROPE
B_HEADS, B_D = 4, 64
B_V = 2 * B_D
C_HEADS, C_KV_HEADS, C_HD = 8, 2, 128

HEAD_PAD = 128
ROW_TILE = 256
POST_TILE = 1024
FLASH_QUERY_LANES = 1024
BOUNDED_KEY_CHUNK = 3328
ONLINE_KEY_CHUNK = 1280
CHUNKS_PER_TRIP = 4
NEG_BIG = -1e30
SCORE_BOUND_LIMIT = 48.0
BOUND_MARGIN = 1.0 + 2.0 ** -6
VMEM_LIMIT_BYTES = 56 * 1024 * 1024


def _silu(v):
    return v / (1.0 + jnp.exp(-v))


def _dot(a, b):
    return jnp.dot(a, b, preferred_element_type=F32)


def _dot_nt(a, b):
    return lax.dot_general(a, b, (((1,), (1,)), ((), ())), preferred_element_type=F32)


def _rms_rows(v, g):
    ms = jnp.mean(v * v, axis=0, keepdims=True)
    return v * lax.rsqrt(ms + EPS) * g


def _rope_rows(v, cos, sin):
    half = v.shape[0] // 2
    x1, x2 = v[:half], v[half:]
    return jnp.concatenate([x1 * cos - x2 * sin, x2 * cos + x1 * sin], axis=0)


def _compiler_params(n_grid):
    return pltpu.CompilerParams(
        dimension_semantics=("arbitrary",) * n_grid,
        vmem_limit_bytes=VMEM_LIMIT_BYTES,
    )


def _mod_kernel(c_ref, w_ref, b_ref, o_ref):
    a = _silu(c_ref[...])
    w = w_ref[...]
    a_hi = a.astype(BF16)
    a_lo = (a - a_hi.astype(F32)).astype(BF16)
    w_hi = w.astype(BF16)
    w_lo = (w - w_hi.astype(F32)).astype(BF16)
    acc = _dot(a_hi, w_hi) + _dot(a_hi, w_lo) + _dot(a_lo, w_hi)
    o_ref[...] = acc + b_ref[...]


def _modulation(cvec, w_mod, b_mod):
    d, d3 = w_mod.shape
    col = 768
    return pl.pallas_call(
        _mod_kernel,
        grid=(d3 // col,),
        in_specs=[
            pl.BlockSpec((8, d), lambda j: (0, 0)),
            pl.BlockSpec((d, col), lambda j: (0, j)),
            pl.BlockSpec((1, col), lambda j: (0, j)),
        ],
        out_specs=pl.BlockSpec((8, col), lambda j: (0, j)),
        out_shape=jax.ShapeDtypeStruct((8, d3), F32),
        compiler_params=_compiler_params(1),
        name="modulation",
    )(cvec, w_mod, b_mod.reshape(1, d3))


def _token_specs(t, d, n_lat_tiles):
    return [pl.BlockSpec((t, d), lambda i: (jnp.minimum(i, n_lat_tiles - 1), 0)),
            pl.BlockSpec((t, d), lambda i: (0, 0))]


def _modulated_rows(x_ref, xc_ref, mod_ref, g_ref, is_ctx):
    d = x_ref.shape[-1]
    x = jnp.where(is_ctx, xc_ref[...], x_ref[...])
    sh = jnp.where(is_ctx, mod_ref[1:2, 0:d], mod_ref[0:1, 0:d])
    sc = jnp.where(is_ctx, mod_ref[1:2, d:2 * d], mod_ref[0:1, d:2 * d])
    ms = jnp.mean(x * x, axis=-1, keepdims=True)
    y = x * lax.rsqrt(ms + EPS) * g_ref[...]
    return (y * (1.0 + sc) + sh).astype(BF16)


def _even_pre_kernel(n_lat_tiles,
                     x_ref, xc_ref, mod_ref, g_ref, wt_ref,
                     gcq_ref, wuq_ref, gckv_ref, wukv_ref, gqa_ref, gka_ref,
                     cosa_ref, sina_ref, gqb_ref, gkb_ref, cosb_ref, sinb_ref,
                     qta_ref, ka_ref, vta_ref, gta_ref,
                     qtb_ref, kb_ref, vtb_ref, gtb_ref):
    is_ctx = pl.program_id(0) >= n_lat_tiles
    hb = _modulated_rows(x_ref, xc_ref, mod_ref, g_ref, is_ctx)
    tok = hb.shape[0]

    o_cq, o_ckv, o_kr = 0, A_Q_LORA, A_Q_LORA + A_KV_LORA
    o_ga = o_kr + A_ROPE
    o_bq = o_ga + A_HEADS * A_V
    o_bk = o_bq + 2 * B_HEADS * B_D
    o_bv = o_bk + 2 * B_HEADS * B_D
    o_gb = o_bv + B_HEADS * B_V
    o_end = o_gb + B_HEADS * B_V

    def proj(lo, hi):
        return _dot_nt(wt_ref[lo:hi, :], hb)

    cq = _rms_rows(proj(o_cq, o_ckv), gcq_ref[...]).astype(BF16)
    q_all = _dot(wuq_ref[...], cq)
    cos_a, sin_a = cosa_ref[...], sina_ref[...]
    q_scale = (A_QK ** -0.5) * LOG2E
    pad_a = jnp.zeros((HEAD_PAD - A_QK, tok), F32)
    for h in range(A_HEADS):
        qh = _rms_rows(q_all[h * A_QK:(h + 1) * A_QK], gqa_ref[...])
        qh = jnp.concatenate(
            [qh[:A_NOPE], _rope_rows(qh[A_NOPE:], cos_a, sin_a), pad_a], axis=0)
        qta_ref[h] = (qh * q_scale).astype(BF16)

    ckv_kr = proj(o_ckv, o_ga)
    ckv = _rms_rows(ckv_kr[:A_KV_LORA], gckv_ref[...]).astype(BF16)
    kr = ckv_kr[A_KV_LORA:]
    kv_all = _dot(wukv_ref[...], ckv)
    for h in range(A_HEADS):
        base = h * (A_NOPE + A_V)
        kh = jnp.concatenate([kv_all[base:base + A_NOPE], kr], axis=0)
        kh = _rms_rows(kh, gka_ref[...])
        kh = jnp.concatenate(
            [kh[:A_NOPE], _rope_rows(kh[A_NOPE:], cos_a, sin_a), pad_a], axis=0)
        ka_ref[h] = jnp.transpose(kh).astype(BF16)
        vh = kv_all[base + A_NOPE:base + A_NOPE + A_V]
        vta_ref[h] = jnp.concatenate([vh, jnp.zeros_like(vh)], axis=0).astype(BF16)

    gta_ref[...] = _silu(proj(o_ga, o_bq))

    cos_b, sin_b = cosb_ref[...], sinb_ref[...]
    qb_scale = (B_D ** -0.5) * LOG2E
    pad_b = jnp.zeros((B_D, tok), F32)
    bq = proj(o_bq, o_bk)
    bk = proj(o_bk, o_bv)
    for h in range(B_HEADS):
        ks = []
        for j in range(2):
            m = 2 * h + j
            qm = _rms_rows(bq[m * B_D:(m + 1) * B_D], gqb_ref[...])
            qm = _rope_rows(qm, cos_b, sin_b) * qb_scale
            parts = [qm, pad_b] if j == 0 else [pad_b, qm]
            qtb_ref[m] = jnp.concatenate(parts, axis=0).astype(BF16)
            km = _rms_rows(bk[m * B_D:(m + 1) * B_D], gkb_ref[...])
            ks.append(_rope_rows(km, cos_b, sin_b))
        kb_ref[h] = jnp.transpose(jnp.concatenate(ks, axis=0)).astype(BF16)
    bv = proj(o_bv, o_gb)
    for h in range(B_HEADS):
        vtb_ref[h] = bv[h * B_V:(h + 1) * B_V].astype(BF16)
    gtb_ref[...] = _silu(proj(o_gb, o_end))


def _bcast_rows(g, tok):
    return jnp.broadcast_to(g.astype(F32)[:, None], (g.shape[0], tok))


def _even_pre(x_lat, x_ctx, mod, norm_g, w_in, a_g_cq, a_w_uq, a_g_ckv, a_w_ukv,
              a_g_qn, a_g_kn, b_g_qn, b_g_kn, rope_a, rope_b):
    n_lat, d = x_lat.shape
    t = ROW_TILE
    seq = n_lat + x_ctx.shape[0]
    n_tiles = seq // t
    e_in = w_in.shape[1]
    const2 = lambda i: (0, 0)
    tok2 = lambda i: (0, i)
    tok3 = lambda i: (0, 0, i)
    row3 = lambda i: (0, i, 0)
    in_specs = _token_specs(t, d, n_lat // t) + [
        pl.BlockSpec((8, 3 * d), const2),
        pl.BlockSpec((1, d), const2),
        pl.BlockSpec((e_in, d), const2),
        pl.BlockSpec((A_Q_LORA, t), const2),
        pl.BlockSpec((A_HEADS * A_QK, A_Q_LORA), const2),
        pl.BlockSpec((A_KV_LORA, t), const2),
        pl.BlockSpec((A_HEADS * (A_NOPE + A_V), A_KV_LORA), const2),
        pl.BlockSpec((A_QK, t), const2),
        pl.BlockSpec((A_QK, t), const2),
        pl.BlockSpec((A_ROPE // 2, t), tok2),
        pl.BlockSpec((A_ROPE // 2, t), tok2),
        pl.BlockSpec((B_D, t), const2),
        pl.BlockSpec((B_D, t), const2),
        pl.BlockSpec((B_D // 2, t), tok2),
        pl.BlockSpec((B_D // 2, t), tok2),
    ]
    out_shape = [
        jax.ShapeDtypeStruct((A_HEADS, HEAD_PAD, seq), BF16),
        jax.ShapeDtypeStruct((A_HEADS, seq, HEAD_PAD), BF16),
        jax.ShapeDtypeStruct((A_HEADS, 2 * A_V, seq), BF16),
        jax.ShapeDtypeStruct((A_HEADS * A_V, seq), F32),
        jax.ShapeDtypeStruct((2 * B_HEADS, HEAD_PAD, seq), BF16),
        jax.ShapeDtypeStruct((B_HEADS, seq, HEAD_PAD), BF16),
        jax.ShapeDtypeStruct((B_HEADS, B_V, seq), BF16),
        jax.ShapeDtypeStruct((B_HEADS * B_V, seq), F32),
    ]
    out_specs = [
        pl.BlockSpec((A_HEADS, HEAD_PAD, t), tok3),
        pl.BlockSpec((A_HEADS, t, HEAD_PAD), row3),
        pl.BlockSpec((A_HEADS, 2 * A_V, t), tok3),
        pl.BlockSpec((A_HEADS * A_V, t), tok2),
        pl.BlockSpec((2 * B_HEADS, HEAD_PAD, t), tok3),
        pl.BlockSpec((B_HEADS, t, HEAD_PAD), row3),
        pl.BlockSpec((B_HEADS, B_V, t), tok3),
        pl.BlockSpec((B_HEADS * B_V, t), tok2),
    ]
    return pl.pallas_call(
        functools.partial(_even_pre_kernel, n_lat // t),
        grid=(n_tiles,),
        in_specs=in_specs,
        out_specs=out_specs,
        out_shape=out_shape,
        compiler_params=_compiler_params(1),
        name="even_pre",
    )(x_lat, x_ctx, mod, norm_g.reshape(1, d), w_in.T.astype(BF16),
      _bcast_rows(a_g_cq, t), a_w_uq.T.astype(BF16),
      _bcast_rows(a_g_ckv, t), a_w_ukv.T.astype(BF16),
      _bcast_rows(a_g_qn, t), _bcast_rows(a_g_kn, t), rope_a[0], rope_a[1],
      _bcast_rows(b_g_qn, t), _bcast_rows(b_g_kn, t), rope_b[0], rope_b[1])


def _odd_pre_kernel(n_lat_tiles,
                    x_ref, xc_ref, mod_ref, g_ref, wt_ref, gq_ref, gk_ref, cos_ref, sin_ref,
                    qt_ref, k_ref, vt_ref, gt_ref):
    is_ctx = pl.program_id(0) >= n_lat_tiles
    hb = _modulated_rows(x_ref, xc_ref, mod_ref, g_ref, is_ctx)
    cos, sin = cos_ref[...], sin_ref[...]
    o_k = C_HEADS * C_HD
    o_v = o_k + C_KV_HEADS * C_HD
    o_g = o_v + C_KV_HEADS * C_HD
    o_end = o_g + C_HEADS * C_HD
    q_scale = (C_HD ** -0.5) * LOG2E
    q = _dot_nt(wt_ref[0:o_k, :], hb)
    for h in range(C_HEADS):
        qh = _rms_rows(q[h * C_HD:(h + 1) * C_HD], gq_ref[...])
        qt_ref[h] = (_rope_rows(qh, cos, sin) * q_scale).astype(BF16)
    kv = _dot_nt(wt_ref[o_k:o_g, :], hb)
    for h in range(C_KV_HEADS):
        kh = _rms_rows(kv[h * C_HD:(h + 1) * C_HD], gk_ref[...])
        k_ref[h] = jnp.transpose(_rope_rows(kh, cos, sin)).astype(BF16)
        vt_ref[h] = kv[(C_KV_HEADS + h) * C_HD:(C_KV_HEADS + h + 1) * C_HD].astype(BF16)
    gt_ref[...] = _silu(_dot_nt(wt_ref[o_g:o_end, :], hb))


def _odd_pre(x_lat, x_ctx, mod, norm_g, w_in, g_qn, g_kn, rope_c):
    n_lat, d = x_lat.shape
    t = ROW_TILE
    seq = n_lat + x_ctx.shape[0]
    o_in = w_in.shape[1]
    const2 = lambda i: (0, 0)
    tok2 = lambda i: (0, i)
    tok3 = lambda i: (0, 0, i)
    in_specs = _token_specs(t, d, n_lat // t) + [
        pl.BlockSpec((8, 3 * d), const2),
        pl.BlockSpec((1, d), const2),
        pl.BlockSpec((o_in, d), const2),
        pl.BlockSpec((C_HD, t), const2),
        pl.BlockSpec((C_HD, t), const2),
        pl.BlockSpec((C_HD // 2, t), tok2),
        pl.BlockSpec((C_HD // 2, t), tok2),
    ]
    out_shape = [
        jax.ShapeDtypeStruct((C_HEADS, C_HD, seq), BF16),
        jax.ShapeDtypeStruct((C_KV_HEADS, seq, C_HD), BF16),
        jax.ShapeDtypeStruct((C_KV_HEADS, C_HD, seq), BF16),
        jax.ShapeDtypeStruct((C_HEADS * C_HD, seq), F32),
    ]
    out_specs = [
        pl.BlockSpec((C_HEADS, C_HD, t), tok3),
        pl.BlockSpec((C_KV_HEADS, t, C_HD), lambda i: (0, i, 0)),
        pl.BlockSpec((C_KV_HEADS, C_HD, t), tok3),
        pl.BlockSpec((C_HEADS * C_HD, t), tok2),
    ]
    return pl.pallas_call(
        functools.partial(_odd_pre_kernel, n_lat // t),
        grid=(seq // t,),
        in_specs=in_specs,
        out_specs=out_specs,
        out_shape=out_shape,
        compiler_params=_compiler_params(1),
        name="odd_pre",
    )(x_lat, x_ctx, mod, norm_g.reshape(1, d), w_in.T.astype(BF16),
      _bcast_rows(g_qn, t), _bcast_rows(g_kn, t), rope_c[0], rope_c[1])


def _flash_streams(qts, k_ref, vt_ref, s_ref, acc_ref, n_chunks, tk):
    n_s = len(qts)
    tq = qts[0].shape[1]
    for st in range(n_s):
        acc_ref[st] = jnp.zeros(acc_ref.shape[1:], F32)

    def scores(slot, c):
        start = pl.multiple_of(c * tk, tk)
        k = k_ref[0, pl.ds(start, tk), :]
        maxes = []
        for st in range(n_s):
            s = _dot(k, qts[st])
            s_ref[st, slot] = s
            maxes.append(jnp.max(s, axis=0, keepdims=True))
        return tuple(maxes)

    def consume(slot, c, state, maxes):
        ms, ls = state
        start = pl.multiple_of(c * tk, tk)
        vt = vt_ref[0, :, pl.ds(start, tk)]
        new_ms, new_ls = [], []
        for st in range(n_s):
            m_new = jnp.maximum(ms[st], maxes[st])
            alpha = jnp.exp2(ms[st] - m_new)
            p = jnp.exp2(s_ref[st, slot] - m_new)
            new_ls.append(alpha * ls[st] + jnp.sum(p, axis=0, keepdims=True))
            acc_ref[st] = alpha * acc_ref[st] + _dot(vt, p.astype(BF16))
            new_ms.append(m_new)
        return tuple(new_ms), tuple(new_ls)

    def step(c, slot, state, mx_cur):
        mx_next = scores(1 - slot, c + 1)
        return consume(slot, c, state, mx_cur), mx_next

    def group(j, carry):
        state, mx = carry
        for u in range(CHUNKS_PER_TRIP):
            state, mx = step(j * CHUNKS_PER_TRIP + u, u % 2, state, mx)
        return state, mx

    state = (tuple(jnp.full((1, tq), NEG_BIG, F32) for _ in range(n_s)),
             tuple(jnp.zeros((1, tq), F32) for _ in range(n_s)))
    mx = scores(0, 0)
    n_trips = (n_chunks - 1) // CHUNKS_PER_TRIP
    state, mx = lax.fori_loop(0, n_trips, group, (state, mx))
    for c in range(n_trips * CHUNKS_PER_TRIP, n_chunks - 1):
        state, mx = step(c, c % 2, state, mx)
    state = consume((n_chunks - 1) % 2, n_chunks - 1, state, mx)
    _, ls = state
    return [acc_ref[st] / ls[st] for st in range(n_s)]


def _bounded_streams(qts, b_all, k_ref, vt_ref, acc_ref, n_chunks, tk):
    n_s = len(qts)
    tq = qts[0].shape[1]
    q_all = jnp.concatenate(qts, axis=1) if n_s > 1 else qts[0]
    l_all = None
    for c in range(n_chunks):
        k = k_ref[0, c * tk:(c + 1) * tk, :]
        vt = vt_ref[0, :, c * tk:(c + 1) * tk]
        p = jnp.exp2(_dot(k, q_all) - b_all)
        p_sum = jnp.sum(p, axis=0, keepdims=True)
        pv = _dot(vt, p.astype(BF16))
        if c == 0:
            l_all = p_sum
            acc_ref[...] = pv
        else:
            l_all = l_all + p_sum
            acc_ref[...] += pv
    o_all = acc_ref[...] / l_all
    return [o_all[:, st * tq:(st + 1) * tq] for st in range(n_s)]


def _attend(bounded, n_s, n_chunks, tk, refs):
    qts = [r[0] for r in refs[:n_s]]
    pos = n_s
    if bounded:
        b_all = refs[pos][...]
        pos += 1
    k_ref, vt_ref = refs[pos:pos + 2]
    pos += 2
    if bounded:
        yt_ref, acc_ref = refs[-2:]
        outs = _bounded_streams(qts, b_all, k_ref, vt_ref, acc_ref, n_chunks, tk)
        return outs, refs[pos:-2], yt_ref
    yt_ref, s_ref, acc_ref = refs[-3:]
    outs = _flash_streams(qts, k_ref, vt_ref, s_ref, acc_ref, n_chunks, tk)
    return outs, refs[pos:-3], yt_ref


def _flash_gate_kernel(bounded, n_chunks, tk, *refs):
    (o,), (gate_ref,), yt_ref = _attend(bounded, 1, n_chunks, tk, refs)
    yt_ref[...] = (o[:gate_ref.shape[0]] * gate_ref[...]).astype(BF16)


def _flash_diff_kernel(lam_init, bounded, n_chunks, tk, *refs):
    (o1, o2), (gate_ref, lam_ref, gout_ref), yt_ref = _attend(bounded, 2, n_chunks, tk, refs)
    lv = lam_ref[...]
    lam = (jnp.exp(jnp.sum(lv[0:1] * lv[1:2], axis=-1, keepdims=True))
           - jnp.exp(jnp.sum(lv[2:3] * lv[3:4], axis=-1, keepdims=True)) + lam_init)
    o = o1 - lam * o2
    o = _rms_rows(o, gout_ref[...]) * (1.0 - lam_init)
    yt_ref[...] = (o * gate_ref[...]).astype(BF16)


def _flash_call(kernel_fn, n_streams, qt, bound, k, vt, gate, extra, *, n_heads, kv_group,
                q_tiles, q_tile0, tq, k_len, k_blk0, tk, name):
    dv = vt.shape[1]
    dv_out = gate.shape[0] // n_heads
    n_chunks = k_len // tk
    bounded = bound is not None
    q_specs = [
        pl.BlockSpec((1, HEAD_PAD, tq),
                     functools.partial(lambda s, h, i: (n_streams * h + s, 0, q_tile0 + i), s))
        for s in range(n_streams)
    ]
    bound_specs = [pl.BlockSpec(bound.shape, lambda h, i: (0, 0))] if bounded else []
    in_specs = q_specs + bound_specs + [
        pl.BlockSpec((1, k_len, HEAD_PAD), lambda h, i: (h // kv_group, k_blk0, 0)),
        pl.BlockSpec((1, dv, k_len), lambda h, i: (h // kv_group, 0, k_blk0)),
        pl.BlockSpec((dv_out, tq), lambda h, i: (h, q_tile0 + i)),
    ] + [pl.BlockSpec(e.shape, lambda h, i: (0, 0)) for e in extra]
    if bounded:
        scratch = [pltpu.VMEM((dv, n_streams * tq), F32)]
    else:
        scratch = [pltpu.VMEM((n_streams, 2, tk, tq), F32),
                   pltpu.VMEM((n_streams, dv, tq), F32)]
    args = [qt] * n_streams + ([bound] if bounded else [])
    return pl.pallas_call(
        functools.partial(kernel_fn, bounded, n_chunks, tk),
        grid=(n_heads, q_tiles),
        in_specs=in_specs,
        out_specs=pl.BlockSpec((dv_out, tq), lambda h, i: (h, i)),
        out_shape=jax.ShapeDtypeStruct((n_heads * dv_out, q_tiles * tq), BF16),
        scratch_shapes=scratch,
        compiler_params=_compiler_params(2),
        name=name,
    )(*args, k, vt, gate, *extra)


def _score_bound(dim, g_q, g_k):
    return (math.sqrt(dim) * LOG2E * BOUND_MARGIN
            * jnp.max(jnp.abs(g_q)) * jnp.max(jnp.abs(g_k))).astype(F32)


def _prefix_flash(kernel_fn, n_streams, qt, k, vt, gate, score_bound, extra, *, n_heads,
                  kv_group, n_lat, n_ctx, update_ctx, name):
    seq = n_lat + n_ctx
    tq = FLASH_QUERY_LANES // n_streams
    if n_lat % tq != 0:
        tq = ROW_TILE
    common = dict(n_heads=n_heads, kv_group=kv_group)
    lat = dict(q_tiles=n_lat // tq, q_tile0=0, tq=tq, k_len=seq, k_blk0=0, **common)

    def key_chunk(size):
        return size if seq % size == 0 else ROW_TILE

    bound = jnp.full((1, n_streams * tq), score_bound, F32)
    ex = extra(tq)

    def bounded(*a):
        return _flash_call(kernel_fn, n_streams, a[0], a[1], *a[2:5], a[5:], name=name,
                           tk=key_chunk(BOUNDED_KEY_CHUNK), **lat)

    def online(*a):
        return _flash_call(kernel_fn, n_streams, a[0], None, *a[2:5], a[5:],
                           name=name + "_online", tk=key_chunk(ONLINE_KEY_CHUNK), **lat)

    y = lax.cond(score_bound <= SCORE_BOUND_LIMIT, bounded, online,
                 qt, bound, k, vt, gate, *ex)
    if not update_ctx:
        return y, None
    y_ctx = _flash_call(kernel_fn, n_streams, qt, None, k, vt, gate, extra(n_ctx),
                        q_tiles=1, q_tile0=n_lat // n_ctx, tq=n_ctx,
                        k_len=n_ctx, k_blk0=n_lat // n_ctx, tk=n_ctx,
                        name=name + "_ctx", **common)
    return y, y_ctx


def _post_kernel(mod_row, n_y, *refs):
    y_refs = refs[:n_y]
    wo_ref, x_ref, mod_ref, o_ref = refs[n_y:]
    d = x_ref.shape[-1]
    acc = None
    off = 0
    for y_ref in y_refs:
        w = y_ref.shape[0]
        part = _dot(wo_ref[:, off:off + w], y_ref[...])
        acc = part if acc is None else acc + part
        off += w
    gate = mod_ref[mod_row:mod_row + 1, 2 * d:3 * d]
    o_ref[...] = x_ref[...] + gate * jnp.transpose(acc)


def _post(ys, w_out_t, x, mod, mod_row, name):
    n_rows, d = x.shape
    t = POST_TILE if n_rows % POST_TILE == 0 else ROW_TILE
    width = w_out_t.shape[1]
    in_specs = [pl.BlockSpec((y.shape[0], t), lambda i: (0, i)) for y in ys] + [
        pl.BlockSpec((d, width), lambda i: (0, 0)),
        pl.BlockSpec((t, d), lambda i: (i, 0)),
        pl.BlockSpec((8, 3 * d), lambda i: (0, 0)),
    ]
    return pl.pallas_call(
        functools.partial(_post_kernel, mod_row, len(ys)),
        grid=(n_rows // t,),
        in_specs=in_specs,
        out_specs=pl.BlockSpec((t, d), lambda i: (i, 0)),
        out_shape=jax.ShapeDtypeStruct((n_rows, d), F32),
        compiler_params=_compiler_params(1),
        name=name,
    )(*ys, w_out_t, x, mod)


def _rope_tables(n_lat, n_ctx, dim):
    tok = np.arange(n_lat)
    rows, cols = tok // GRID_W, tok % GRID_W
    n_freq = dim // 4
    inv = ROPE_THETA ** (-np.arange(n_freq) / n_freq)
    ang = np.concatenate([rows[:, None] * inv, cols[:, None] * inv], axis=-1)
    ang = np.concatenate([ang, np.zeros((n_ctx, dim // 2))], axis=0)
    return (jnp.asarray(np.cos(ang).T, dtype=F32), jnp.asarray(np.sin(ang).T, dtype=F32))


def kernel(x, c, ctx, c_ctx, e_norm_g, e_w_mod, e_b_mod, e_w_in, a_g_cq, a_w_uq, a_g_ckv, a_w_ukv, a_g_qn, a_g_kn, b_g_qn, b_g_kn, b_lam_q1, b_lam_k1, b_lam_q2, b_lam_k2, b_g_out, e_w_out, o_norm_g, o_w_mod, o_b_mod, o_w_in, c_g_qn, c_g_kn, o_w_out):
    batch, n_lat, d = x.shape
    n_ctx = ctx.shape[1]
    assert batch == 1 and n_ctx == ROW_TILE and n_lat % ROW_TILE == 0
    depth = e_norm_g.shape[0] + o_norm_g.shape[0]

    cvec = jnp.zeros((8, d), F32).at[0].set(c[0]).at[1].set(c_ctx)
    rope_a = _rope_tables(n_lat, n_ctx, A_ROPE)
    rope_b = _rope_tables(n_lat, n_ctx, B_D)
    rope_c = _rope_tables(n_lat, n_ctx, C_HD)

    x_lat, x_ctx = x[0], ctx[0]
    for layer in range(depth):
        i = layer // 2
        update_ctx = layer < depth - 1
        pf = dict(n_lat=n_lat, n_ctx=n_ctx, update_ctx=update_ctx)
        if layer % 2 == 0:
            lam_init = 0.8 - 0.6 * math.exp(-0.3 * layer)
            mod = _modulation(cvec, e_w_mod[i], e_b_mod[i])
            qta, ka, vta, gta, qtb, kb, vtb, gtb = _even_pre(
                x_lat, x_ctx, mod, e_norm_g[i], e_w_in[i], a_g_cq[i], a_w_uq[i],
                a_g_ckv[i], a_w_ukv[i], a_g_qn[i], a_g_kn[i], b_g_qn[i], b_g_kn[i],
                rope_a, rope_b)
            ya = _prefix_flash(_flash_gate_kernel, 1, qta, ka, vta, gta,
                               _score_bound(A_QK, a_g_qn[i], a_g_kn[i]),
                               lambda tq: [], n_heads=A_HEADS, kv_group=1,
                               name="flash_mla", **pf)
            lam_vecs = jnp.stack([b_lam_q1[i], b_lam_k1[i], b_lam_q2[i], b_lam_k2[i]]).astype(F32)
            yb = _prefix_flash(
                functools.partial(_flash_diff_kernel, lam_init), 2, qtb, kb, vtb, gtb,
                _score_bound(B_D, b_g_qn[i], b_g_kn[i]),
                lambda tq: [lam_vecs, _bcast_rows(b_g_out[i], tq)],
                n_heads=B_HEADS, kv_group=1, name="flash_diff", **pf)
            ys, w_out = [ya, yb], e_w_out[i]
        else:
            mod = _modulation(cvec, o_w_mod[i], o_b_mod[i])
            qtc, kc, vtc, gtc = _odd_pre(
                x_lat, x_ctx, mod, o_norm_g[i], o_w_in[i], c_g_qn[i], c_g_kn[i], rope_c)
            ys = [_prefix_flash(_flash_gate_kernel, 1, qtc, kc, vtc, gtc,
                                _score_bound(C_HD, c_g_qn[i], c_g_kn[i]),
                                lambda tq: [], n_heads=C_HEADS,
                                kv_group=C_HEADS // C_KV_HEADS, name="flash_gqa", **pf)]
            w_out = o_w_out[i]
        w_out_t = w_out.T.astype(BF16)
        if update_ctx:
            x_ctx = _post([y[1] for y in ys], w_out_t, x_ctx, mod, 1, "post_ctx")
        x_lat = _post([y[0] for y in ys], w_out_t, x_lat, mod, 0, "post")
    return x_lat.reshape(batch, n_lat, d)
```

```python
import functools
import math

import jax
import jax.numpy as jnp
import numpy as np
from jax import lax
from jax.experimental import pallas as pl
from jax.experimental.pallas import tpu as pltpu

F32 = jnp.float32
BF16 = jnp.bfloat16

GRID_W = 64
ROPE_THETA = 10000.0
EPS = 1e-6
LOG2E = math.log2(math.e)

A_HEADS, A_Q_LORA, A_KV_LORA, A_NOPE, A_ROPE, A_V = 8, 256, 128, 64, 32, 64
A_QK = A_NOPE + A_ROPE
B_HEADS, B_D = 4, 64
B_V = 2 * B_D
C_HEADS, C_KV_HEADS, C_HD = 8, 2, 128

HEAD_PAD = 128
ROW_TILE = 256
POST_TILE = 1024
FLASH_QUERY_LANES = 1024
BOUNDED_KEY_CHUNK = 3328
ONLINE_KEY_CHUNK = 1280
CHUNKS_PER_TRIP = 4
NEG_BIG = -1e30
SCORE_BOUND_LIMIT = 48.0
BOUND_MARGIN = 1.0 + 2.0 ** -6
VMEM_LIMIT_BYTES = 56 * 1024 * 1024


def _silu(v):
    return v / (1.0 + jnp.exp(-v))


def _dot(a, b):
    return jnp.dot(a, b, preferred_element_type=F32)


def _dot_nt(a, b):
    return lax.dot_general(a, b, (((1,), (1,)), ((), ())), preferred_element_type=F32)


def _rms_rows(v, g):
    ms = jnp.mean(v * v, axis=0, keepdims=True)
    return v * lax.rsqrt(ms + EPS) * g


def _rope_rows(v, cos, sin):
    half = v.shape[0] // 2
    x1, x2 = v[:half], v[half:]
    return jnp.concatenate([x1 * cos - x2 * sin, x2 * cos + x1 * sin], axis=0)


def _compiler_params(n_grid):
    return pltpu.CompilerParams(
        dimension_semantics=("arbitrary",) * n_grid,
        vmem_limit_bytes=VMEM_LIMIT_BYTES,
    )


def _mod_kernel(c_ref, w_ref, b_ref, o_ref):
    a = _silu(c_ref[...])
    w = w_ref[...]
    a_hi = a.astype(BF16)
    a_lo = (a - a_hi.astype(F32)).astype(BF16)
    w_hi = w.astype(BF16)
    w_lo = (w - w_hi.astype(F32)).astype(BF16)
    acc = _dot(a_hi, w_hi) + _dot(a_hi, w_lo) + _dot(a_lo, w_hi)
    o_ref[...] = acc + b_ref[...]


def _modulation(cvec, w_mod, b_mod):
    d, d3 = w_mod.shape
    col = 768
    return pl.pallas_call(
        _mod_kernel,
        grid=(d3 // col,),
        in_specs=[
            pl.BlockSpec((8, d), lambda j: (0, 0)),
            pl.BlockSpec((d, col), lambda j: (0, j)),
            pl.BlockSpec((1, col), lambda j: (0, j)),
        ],
        out_specs=pl.BlockSpec((8, col), lambda j: (0, j)),
        out_shape=jax.ShapeDtypeStruct((8, d3), F32),
        compiler_params=_compiler_params(1),
        name="modulation",
    )(cvec, w_mod, b_mod.reshape(1, d3))


def _token_specs(t, d, n_lat_tiles):
    return [pl.BlockSpec((t, d), lambda i: (jnp.minimum(i, n_lat_tiles - 1), 0)),
            pl.BlockSpec((t, d), lambda i: (0, 0))]


def _modulated_rows(x_ref, xc_ref, mod_ref, g_ref, is_ctx):
    return _modulate(jnp.where(is_ctx, xc_ref[...], x_ref[...]), mod_ref, g_ref, is_ctx)


def _modulate(x, mod_ref, g_ref, is_ctx):
    d = x.shape[-1]
    sh = jnp.where(is_ctx, mod_ref[1:2, 0:d], mod_ref[0:1, 0:d])
    sc = jnp.where(is_ctx, mod_ref[1:2, d:2 * d], mod_ref[0:1, d:2 * d])
    ms = jnp.mean(x * x, axis=-1, keepdims=True)
    y = x * lax.rsqrt(ms + EPS) * g_ref[...]
    return (y * (1.0 + sc) + sh).astype(BF16)


def _even_pre_kernel(n_lat_tiles,
                     x_ref, xc_ref, mod_ref, g_ref, wt_ref,
                     gcq_ref, wuq_ref, gckv_ref, wukv_ref, gqa_ref, gka_ref,
                     cosa_ref, sina_ref, gqb_ref, gkb_ref, cosb_ref, sinb_ref,
                     qta_ref, ka_ref, vta_ref, gta_ref,
                     qtb_ref, kb_ref, vtb_ref, gtb_ref):
    is_ctx = pl.program_id(0) >= n_lat_tiles
    hb = _modulated_rows(x_ref, xc_ref, mod_ref, g_ref, is_ctx)
    tok = hb.shape[0]

    o_cq, o_ckv, o_kr = 0, A_Q_LORA, A_Q_LORA + A_KV_LORA
    o_ga = o_kr + A_ROPE
    o_bq = o_ga + A_HEADS * A_V
    o_bk = o_bq + 2 * B_HEADS * B_D
    o_bv = o_bk + 2 * B_HEADS * B_D
    o_gb = o_bv + B_HEADS * B_V
    o_end = o_gb + B_HEADS * B_V

    def proj(lo, hi):
        return _dot_nt(wt_ref[lo:hi, :], hb)

    cq = _rms_rows(proj(o_cq, o_ckv), gcq_ref[...]).astype(BF16)
    q_all = _dot(wuq_ref[...], cq)
    cos_a, sin_a = cosa_ref[...], sina_ref[...]
    q_scale = (A_QK ** -0.5) * LOG2E
    pad_a = jnp.zeros((HEAD_PAD - A_QK, tok), F32)
    for h in range(A_HEADS):
        qh = _rms_rows(q_all[h * A_QK:(h + 1) * A_QK], gqa_ref[...])
        qh = jnp.concatenate(
            [qh[:A_NOPE], _rope_rows(qh[A_NOPE:], cos_a, sin_a), pad_a], axis=0)
        qta_ref[h] = (qh * q_scale).astype(BF16)

    ckv_kr = proj(o_ckv, o_ga)
    ckv = _rms_rows(ckv_kr[:A_KV_LORA], gckv_ref[...]).astype(BF16)
    kr = ckv_kr[A_KV_LORA:]
    kv_all = _dot(wukv_ref[...], ckv)
    for h in range(A_HEADS):
        base = h * (A_NOPE + A_V)
        kh = jnp.concatenate([kv_all[base:base + A_NOPE], kr], axis=0)
        kh = _rms_rows(kh, gka_ref[...])
        kh = jnp.concatenate(
            [kh[:A_NOPE], _rope_rows(kh[A_NOPE:], cos_a, sin_a), pad_a], axis=0)
        ka_ref[h] = jnp.transpose(kh).astype(BF16)
        vh = kv_all[base + A_NOPE:base + A_NOPE + A_V]
        vta_ref[h] = jnp.concatenate([vh, jnp.zeros_like(vh)], axis=0).astype(BF16)

    gta_ref[...] = _silu(proj(o_ga, o_bq))

    cos_b, sin_b = cosb_ref[...], sinb_ref[...]
    qb_scale = (B_D ** -0.5) * LOG2E
    pad_b = jnp.zeros((B_D, tok), F32)
    bq = proj(o_bq, o_bk)
    bk = proj(o_bk, o_bv)
    for h in range(B_HEADS):
        ks = []
        for j in range(2):
            m = 2 * h + j
            qm = _rms_rows(bq[m * B_D:(m + 1) * B_D], gqb_ref[...])
            qm = _rope_rows(qm, cos_b, sin_b) * qb_scale
            parts = [qm, pad_b] if j == 0 else [pad_b, qm]
            qtb_ref[m] = jnp.concatenate(parts, axis=0).astype(BF16)
            km = _rms_rows(bk[m * B_D:(m + 1) * B_D], gkb_ref[...])
            ks.append(_rope_rows(km, cos_b, sin_b))
        kb_ref[h] = jnp.transpose(jnp.concatenate(ks, axis=0)).astype(BF16)
    bv = proj(o_bv, o_gb)
    for h in range(B_HEADS):
        vtb_ref[h] = bv[h * B_V:(h + 1) * B_V].astype(BF16)
    gtb_ref[...] = _silu(proj(o_gb, o_end))


def _bcast_rows(g, tok):
    return jnp.broadcast_to(g.astype(F32)[:, None], (g.shape[0], tok))


def _even_pre(x_lat, x_ctx, mod, norm_g, w_in, a_g_cq, a_w_uq, a_g_ckv, a_w_ukv,
              a_g_qn, a_g_kn, b_g_qn, b_g_kn, rope_a, rope_b):
    n_lat, d = x_lat.shape
    t = ROW_TILE
    seq = n_lat + x_ctx.shape[0]
    n_tiles = seq // t
    e_in = w_in.shape[1]
    const2 = lambda i: (0, 0)
    tok2 = lambda i: (0, i)
    tok3 = lambda i: (0, 0, i)
    row3 = lambda i: (0, i, 0)
    in_specs = _token_specs(t, d, n_lat // t) + [
        pl.BlockSpec((8, 3 * d), const2),
        pl.BlockSpec((1, d), const2),
        pl.BlockSpec((e_in, d), const2),
        pl.BlockSpec((A_Q_LORA, t), const2),
        pl.BlockSpec((A_HEADS * A_QK, A_Q_LORA), const2),
        pl.BlockSpec((A_KV_LORA, t), const2),
        pl.BlockSpec((A_HEADS * (A_NOPE + A_V), A_KV_LORA), const2),
        pl.BlockSpec((A_QK, t), const2),
        pl.BlockSpec((A_QK, t), const2),
        pl.BlockSpec((A_ROPE // 2, t), tok2),
        pl.BlockSpec((A_ROPE // 2, t), tok2),
        pl.BlockSpec((B_D, t), const2),
        pl.BlockSpec((B_D, t), const2),
        pl.BlockSpec((B_D // 2, t), tok2),
        pl.BlockSpec((B_D // 2, t), tok2),
    ]
    out_shape = [
        jax.ShapeDtypeStruct((A_HEADS, HEAD_PAD, seq), BF16),
        jax.ShapeDtypeStruct((A_HEADS, seq, HEAD_PAD), BF16),
        jax.ShapeDtypeStruct((A_HEADS, 2 * A_V, seq), BF16),
        jax.ShapeDtypeStruct((A_HEADS * A_V, seq), F32),
        jax.ShapeDtypeStruct((2 * B_HEADS, HEAD_PAD, seq), BF16),
        jax.ShapeDtypeStruct((B_HEADS, seq, HEAD_PAD), BF16),
        jax.ShapeDtypeStruct((B_HEADS, B_V, seq), BF16),
        jax.ShapeDtypeStruct((B_HEADS * B_V, seq), F32),
    ]
    out_specs = [
        pl.BlockSpec((A_HEADS, HEAD_PAD, t), tok3),
        pl.BlockSpec((A_HEADS, t, HEAD_PAD), row3),
        pl.BlockSpec((A_HEADS, 2 * A_V, t), tok3),
        pl.BlockSpec((A_HEADS * A_V, t), tok2),
        pl.BlockSpec((2 * B_HEADS, HEAD_PAD, t), tok3),
        pl.BlockSpec((B_HEADS, t, HEAD_PAD), row3),
        pl.BlockSpec((B_HEADS, B_V, t), tok3),
        pl.BlockSpec((B_HEADS * B_V, t), tok2),
    ]
    return pl.pallas_call(
        functools.partial(_even_pre_kernel, n_lat // t),
        grid=(n_tiles,),
        in_specs=in_specs,
        out_specs=out_specs,
        out_shape=out_shape,
        compiler_params=_compiler_params(1),
        name="even_pre",
    )(x_lat, x_ctx, mod, norm_g.reshape(1, d), w_in.T.astype(BF16),
      _bcast_rows(a_g_cq, t), a_w_uq.T.astype(BF16),
      _bcast_rows(a_g_ckv, t), a_w_ukv.T.astype(BF16),
      _bcast_rows(a_g_qn, t), _bcast_rows(a_g_kn, t), rope_a[0], rope_a[1],
      _bcast_rows(b_g_qn, t), _bcast_rows(b_g_kn, t), rope_b[0], rope_b[1])


def _odd_pre_kernel(n_lat_tiles,
                    x_ref, xc_ref, mod_ref, g_ref, wt_ref, gq_ref, gk_ref, cos_ref, sin_ref,
                    qt_ref, k_ref, vt_ref, gt_ref):
    is_ctx = pl.program_id(0) >= n_lat_tiles
    hb = _modulated_rows(x_ref, xc_ref, mod_ref, g_ref, is_ctx)
    _odd_heads(hb, wt_ref, gq_ref, gk_ref, cos_ref, sin_ref, qt_ref, k_ref, vt_ref, gt_ref)


def _odd_heads(hb, wt_ref, gq_ref, gk_ref, cos_ref, sin_ref, qt_ref, k_ref, vt_ref, gt_ref):
    cos, sin = cos_ref[...], sin_ref[...]
    o_k = C_HEADS * C_HD
    o_v = o_k + C_KV_HEADS * C_HD
    o_g = o_v + C_KV_HEADS * C_HD
    o_end = o_g + C_HEADS * C_HD
    q_scale = (C_HD ** -0.5) * LOG2E
    q = _dot_nt(wt_ref[0:o_k, :], hb)
    for h in range(C_HEADS):
        qh = _rms_rows(q[h * C_HD:(h + 1) * C_HD], gq_ref[...])
        qt_ref[h] = (_rope_rows(qh, cos, sin) * q_scale).astype(BF16)
    kv = _dot_nt(wt_ref[o_k:o_g, :], hb)
    for h in range(C_KV_HEADS):
        kh = _rms_rows(kv[h * C_HD:(h + 1) * C_HD], gk_ref[...])
        k_ref[h] = jnp.transpose(_rope_rows(kh, cos, sin)).astype(BF16)
        vt_ref[h] = kv[(C_KV_HEADS + h) * C_HD:(C_KV_HEADS + h + 1) * C_HD].astype(BF16)
    gt_ref[...] = _silu(_dot_nt(wt_ref[o_g:o_end, :], hb))


def _post_odd_pre_kernel(n_lat_tiles, n_y, *refs):
    yl_refs, yc_refs = refs[:n_y], refs[n_y:2 * n_y]
    (wo_ref, x_ref, xc_ref, mod_prev_ref, mod_ref, g_ref, wt_ref, gq_ref, gk_ref,
     cos_ref, sin_ref, x_out_ref, qt_ref, k_ref, vt_ref, gt_ref, x_scr) = refs[2 * n_y:]
    d = x_ref.shape[-1]
    i = pl.program_id(0)

    @pl.when(i == 0)
    def _():
        x_scr[1] = jnp.zeros(x_scr.shape[1:], F32)

    hb = _modulate(x_scr[(i + 1) % 2], mod_ref, g_ref, i - 1 >= n_lat_tiles)
    _odd_heads(hb, wt_ref, gq_ref, gk_ref, cos_ref, sin_ref, qt_ref, k_ref, vt_ref, gt_ref)

    is_ctx = i >= n_lat_tiles
    acc = None
    off = 0
    for yl_ref, yc_ref in zip(yl_refs, yc_refs):
        w = yl_ref.shape[0]
        y = jnp.where(is_ctx, yc_ref[...], yl_ref[...])
        part = _dot(wo_ref[:, off:off + w], y)
        acc = part if acc is None else acc + part
        off += w
    gate = jnp.where(is_ctx, mod_prev_ref[1:2, 2 * d:3 * d], mod_prev_ref[0:1, 2 * d:3 * d])
    x = jnp.where(is_ctx, xc_ref[...], x_ref[...]) + gate * jnp.transpose(acc)
    x_out_ref[...] = x
    x_scr[i % 2] = x


def _odd_pre(x_lat, x_ctx, mod, norm_g, w_in, g_qn, g_kn, rope_c, prev=None):
    n_lat, d = x_lat.shape
    t = ROW_TILE
    seq = n_lat + x_ctx.shape[0]
    o_in = w_in.shape[1]
    n_lat_tiles = n_lat // t
    n_tiles = seq // t
    lag = 0 if prev is None else 1

    def tile(i):
        return jnp.maximum(i - lag, 0)

    const2 = lambda i: (0, 0)
    tok2 = lambda i: (0, tile(i))
    tok3 = lambda i: (0, 0, tile(i))
    in_specs = _token_specs(t, d, n_lat_tiles) + [
        pl.BlockSpec((8, 3 * d), const2),
        pl.BlockSpec((1, d), const2),
        pl.BlockSpec((o_in, d), const2),
        pl.BlockSpec((C_HD, t), const2),
        pl.BlockSpec((C_HD, t), const2),
        pl.BlockSpec((C_HD // 2, t), tok2),
        pl.BlockSpec((C_HD // 2, t), tok2),
    ]
    args = [x_lat, x_ctx, mod, norm_g.reshape(1, d), w_in.T.astype(BF16),
            _bcast_rows(g_qn, t), _bcast_rows(g_kn, t), rope_c[0], rope_c[1]]
    kernel_fn = functools.partial(_odd_pre_kernel, n_lat_tiles)
    name = "odd_pre"
    if prev is not None:
        ys, w_out_t, mod_prev = prev
        y_specs = ([pl.BlockSpec((y[0].shape[0], t),
                                 lambda i: (0, jnp.minimum(i, n_lat_tiles - 1))) for y in ys]
                   + [pl.BlockSpec((y[1].shape[0], t), const2) for y in ys])
        in_specs = (y_specs + [pl.BlockSpec(w_out_t.shape, const2)] + in_specs[:2]
                    + [pl.BlockSpec((8, 3 * d), const2)] + in_specs[2:])
        args = ([y[0] for y in ys] + [y[1] for y in ys] + [w_out_t] + args[:2]
                + [mod_prev] + args[2:])
        kernel_fn = functools.partial(_post_odd_pre_kernel, n_lat_tiles, len(ys))
        name = "post_odd_pre"
    out_shape = [
        jax.ShapeDtypeStruct((C_HEADS, C_HD, seq), BF16),
        jax.ShapeDtypeStruct((C_KV_HEADS, seq, C_HD), BF16),
        jax.ShapeDtypeStruct((C_KV_HEADS, C_HD, seq), BF16),
        jax.ShapeDtypeStruct((C_HEADS * C_HD, seq), F32),
    ]
    out_specs = [
        pl.BlockSpec((C_HEADS, C_HD, t), tok3),
        pl.BlockSpec((C_KV_HEADS, t, C_HD), lambda i: (0, tile(i), 0)),
        pl.BlockSpec((C_KV_HEADS, C_HD, t), tok3),
        pl.BlockSpec((C_HEADS * C_HD, t), tok2),
    ]
    scratch = []
    if prev is not None:
        out_shape = [jax.ShapeDtypeStruct((seq, d), F32)] + out_shape
        out_specs = [pl.BlockSpec((t, d), lambda i: (jnp.minimum(i, n_tiles - 1), 0))] + out_specs
        scratch = [pltpu.VMEM((2, t, d), F32)]
    return pl.pallas_call(
        kernel_fn,
        grid=(n_tiles + lag,),
        in_specs=in_specs,
        out_specs=out_specs,
        out_shape=out_shape,
        scratch_shapes=scratch,
        compiler_params=_compiler_params(1),
        name=name,
    )(*args)


def _flash_streams(qts, k_ref, vt_ref, s_ref, acc_ref, n_chunks, tk):
    n_s = len(qts)
    tq = qts[0].shape[1]
    for st in range(n_s):
        acc_ref[st] = jnp.zeros(acc_ref.shape[1:], F32)

    def scores(slot, c):
        start = pl.multiple_of(c * tk, tk)
        k = k_ref[0, pl.ds(start, tk), :]
        maxes = []
        for st in range(n_s):
            s = _dot(k, qts[st])
            s_ref[st, slot] = s
            maxes.append(jnp.max(s, axis=0, keepdims=True))
        return tuple(maxes)

    def consume(slot, c, state, maxes):
        ms, ls = state
        start = pl.multiple_of(c * tk, tk)
        vt = vt_ref[0, :, pl.ds(start, tk)]
        new_ms, new_ls = [], []
        for st in range(n_s):
            m_new = jnp.maximum(ms[st], maxes[st])
            alpha = jnp.exp2(ms[st] - m_new)
            p = jnp.exp2(s_ref[st, slot] - m_new)
            new_ls.append(alpha * ls[st] + jnp.sum(p, axis=0, keepdims=True))
            acc_ref[st] = alpha * acc_ref[st] + _dot(vt, p.astype(BF16))
            new_ms.append(m_new)
        return tuple(new_ms), tuple(new_ls)

    def step(c, slot, state, mx_cur):
        mx_next = scores(1 - slot, c + 1)
        return consume(slot, c, state, mx_cur), mx_next

    def group(j, carry):
        state, mx = carry
        for u in range(CHUNKS_PER_TRIP):
            state, mx = step(j * CHUNKS_PER_TRIP + u, u % 2, state, mx)
        return state, mx

    state = (tuple(jnp.full((1, tq), NEG_BIG, F32) for _ in range(n_s)),
             tuple(jnp.zeros((1, tq), F32) for _ in range(n_s)))
    mx = scores(0, 0)
    n_trips = (n_chunks - 1) // CHUNKS_PER_TRIP
    state, mx = lax.fori_loop(0, n_trips, group, (state, mx))
    for c in range(n_trips * CHUNKS_PER_TRIP, n_chunks - 1):
        state, mx = step(c, c % 2, state, mx)
    state = consume((n_chunks - 1) % 2, n_chunks - 1, state, mx)
    _, ls = state
    return [acc_ref[st] / ls[st] for st in range(n_s)]


def _bounded_streams(qts, b_all, k_ref, vt_ref, acc_ref, n_chunks, tk):
    n_s = len(qts)
    tq = qts[0].shape[1]
    q_all = jnp.concatenate(qts, axis=1) if n_s > 1 else qts[0]
    l_all = None
    for c in range(n_chunks):
        k = k_ref[0, c * tk:(c + 1) * tk, :]
        vt = vt_ref[0, :, c * tk:(c + 1) * tk]
        p = jnp.exp2(_dot(k, q_all) - b_all)
        p_sum = jnp.sum(p, axis=0, keepdims=True)
        pv = _dot(vt, p.astype(BF16))
        if c == 0:
            l_all = p_sum
            acc_ref[...] = pv
        else:
            l_all = l_all + p_sum
            acc_ref[...] += pv
    o_all = acc_ref[...] / l_all
    return [o_all[:, st * tq:(st + 1) * tq] for st in range(n_s)]


def _attend(bounded, n_s, n_chunks, tk, refs):
    qts = [r[0] for r in refs[:n_s]]
    pos = n_s
    if bounded:
        b_all = refs[pos][...]
        pos += 1
    k_ref, vt_ref = refs[pos:pos + 2]
    pos += 2
    if bounded:
        yt_ref, acc_ref = refs[-2:]
        outs = _bounded_streams(qts, b_all, k_ref, vt_ref, acc_ref, n_chunks, tk)
        return outs, refs[pos:-2], yt_ref
    yt_ref, s_ref, acc_ref = refs[-3:]
    outs = _flash_streams(qts, k_ref, vt_ref, s_ref, acc_ref, n_chunks, tk)
    return outs, refs[pos:-3], yt_ref


def _flash_gate_kernel(bounded, n_chunks, tk, *refs):
    (o,), (gate_ref,), yt_ref = _attend(bounded, 1, n_chunks, tk, refs)
    yt_ref[...] = (o[:gate_ref.shape[0]] * gate_ref[...]).astype(BF16)


def _flash_diff_kernel(lam_init, bounded, n_chunks, tk, *refs):
    (o1, o2), (gate_ref, lam_ref, gout_ref), yt_ref = _attend(bounded, 2, n_chunks, tk, refs)
    lv = lam_ref[...]
    lam = (jnp.exp(jnp.sum(lv[0:1] * lv[1:2], axis=-1, keepdims=True))
           - jnp.exp(jnp.sum(lv[2:3] * lv[3:4], axis=-1, keepdims=True)) + lam_init)
    o = o1 - lam * o2
    o = _rms_rows(o, gout_ref[...]) * (1.0 - lam_init)
    yt_ref[...] = (o * gate_ref[...]).astype(BF16)


def _flash_call(kernel_fn, n_streams, qt, bound, k, vt, gate, extra, *, n_heads, kv_group,
                q_tiles, q_tile0, tq, k_len, k_blk0, tk, name):
    dv = vt.shape[1]
    dv_out = gate.shape[0] // n_heads
    n_chunks = k_len // tk
    bounded = bound is not None
    q_specs = [
        pl.BlockSpec((1, HEAD_PAD, tq),
                     functools.partial(lambda s, h, i: (n_streams * h + s, 0, q_tile0 + i), s))
        for s in range(n_streams)
    ]
    bound_specs = [pl.BlockSpec(bound.shape, lambda h, i: (0, 0))] if bounded else []
    in_specs = q_specs + bound_specs + [
        pl.BlockSpec((1, k_len, HEAD_PAD), lambda h, i: (h // kv_group, k_blk0, 0)),
        pl.BlockSpec((1, dv, k_len), lambda h, i: (h // kv_group, 0, k_blk0)),
        pl.BlockSpec((dv_out, tq), lambda h, i: (h, q_tile0 + i)),
    ] + [pl.BlockSpec(e.shape, lambda h, i: (0, 0)) for e in extra]
    if bounded:
        scratch = [pltpu.VMEM((dv, n_streams * tq), F32)]
    else:
        scratch = [pltpu.VMEM((n_streams, 2, tk, tq), F32),
                   pltpu.VMEM((n_streams, dv, tq), F32)]
    args = [qt] * n_streams + ([bound] if bounded else [])
    return pl.pallas_call(
        functools.partial(kernel_fn, bounded, n_chunks, tk),
        grid=(n_heads, q_tiles),
        in_specs=in_specs,
        out_specs=pl.BlockSpec((dv_out, tq), lambda h, i: (h, i)),
        out_shape=jax.ShapeDtypeStruct((n_heads * dv_out, q_tiles * tq), BF16),
        scratch_shapes=scratch,
        compiler_params=_compiler_params(2),
        name=name,
    )(*args, k, vt, gate, *extra)


def _score_bound(dim, g_q, g_k):
    return (math.sqrt(dim) * LOG2E * BOUND_MARGIN
            * jnp.max(jnp.abs(g_q)) * jnp.max(jnp.abs(g_k))).astype(F32)


def _prefix_flash(kernel_fn, n_streams, qt, k, vt, gate, score_bound, extra, *, n_heads,
                  kv_group, n_lat, n_ctx, update_ctx, name):
    seq = n_lat + n_ctx
    tq = FLASH_QUERY_LANES // n_streams
    if n_lat % tq != 0:
        tq = ROW_TILE
    common = dict(n_heads=n_heads, kv_group=kv_group)
    lat = dict(q_tiles=n_lat // tq, q_tile0=0, tq=tq, k_len=seq, k_blk0=0, **common)

    def key_chunk(size):
        return size if seq % size == 0 else ROW_TILE

    bound = jnp.full((1, n_streams * tq), score_bound, F32)
    ex = extra(tq)

    def bounded(*a):
        return _flash_call(kernel_fn, n_streams, a[0], a[1], *a[2:5], a[5:], name=name,
                           tk=key_chunk(BOUNDED_KEY_CHUNK), **lat)

    def online(*a):
        return _flash_call(kernel_fn, n_streams, a[0], None, *a[2:5], a[5:],
                           name=name + "_online", tk=key_chunk(ONLINE_KEY_CHUNK), **lat)

    y = lax.cond(score_bound <= SCORE_BOUND_LIMIT, bounded, online,
                 qt, bound, k, vt, gate, *ex)
    if not update_ctx:
        return y, None
    y_ctx = _flash_call(kernel_fn, n_streams, qt, None, k, vt, gate, extra(n_ctx),
                        q_tiles=1, q_tile0=n_lat // n_ctx, tq=n_ctx,
                        k_len=n_ctx, k_blk0=n_lat // n_ctx, tk=n_ctx,
                        name=name + "_ctx", **common)
    return y, y_ctx


def _post_kernel(mod_row, n_y, *refs):
    y_refs = refs[:n_y]
    wo_ref, x_ref, mod_ref, o_ref = refs[n_y:]
    d = x_ref.shape[-1]
    acc = None
    off = 0
    for y_ref in y_refs:
        w = y_ref.shape[0]
        part = _dot(wo_ref[:, off:off + w], y_ref[...])
        acc = part if acc is None else acc + part
        off += w
    gate = mod_ref[mod_row:mod_row + 1, 2 * d:3 * d]
    o_ref[...] = x_ref[...] + gate * jnp.transpose(acc)


def _post(ys, w_out_t, x, mod, mod_row, name, n_rows=None):
    d = x.shape[1]
    n_rows = x.shape[0] if n_rows is None else n_rows
    t = POST_TILE if n_rows % POST_TILE == 0 else ROW_TILE
    width = w_out_t.shape[1]
    in_specs = [pl.BlockSpec((y.shape[0], t), lambda i: (0, i)) for y in ys] + [
        pl.BlockSpec((d, width), lambda i: (0, 0)),
        pl.BlockSpec((t, d), lambda i: (i, 0)),
        pl.BlockSpec((8, 3 * d), lambda i: (0, 0)),
    ]
    return pl.pallas_call(
        functools.partial(_post_kernel, mod_row, len(ys)),
        grid=(n_rows // t,),
        in_specs=in_specs,
        out_specs=pl.BlockSpec((t, d), lambda i: (i, 0)),
        out_shape=jax.ShapeDtypeStruct((n_rows, d), F32),
        compiler_params=_compiler_params(1),
        name=name,
    )(*ys, w_out_t, x, mod)


def _rope_tables(n_lat, n_ctx, dim):
    tok = np.arange(n_lat)
    rows, cols = tok // GRID_W, tok % GRID_W
    n_freq = dim // 4
    inv = ROPE_THETA ** (-np.arange(n_freq) / n_freq)
    ang = np.concatenate([rows[:, None] * inv, cols[:, None] * inv], axis=-1)
    ang = np.concatenate([ang, np.zeros((n_ctx, dim // 2))], axis=0)
    return (jnp.asarray(np.cos(ang).T, dtype=F32), jnp.asarray(np.sin(ang).T, dtype=F32))


def kernel(x, c, ctx, c_ctx, e_norm_g, e_w_mod, e_b_mod, e_w_in, a_g_cq, a_w_uq, a_g_ckv, a_w_ukv, a_g_qn, a_g_kn, b_g_qn, b_g_kn, b_lam_q1, b_lam_k1, b_lam_q2, b_lam_k2, b_g_out, e_w_out, o_norm_g, o_w_mod, o_b_mod, o_w_in, c_g_qn, c_g_kn, o_w_out):
    batch, n_lat, d = x.shape
    n_ctx = ctx.shape[1]
    assert batch == 1 and n_ctx == ROW_TILE and n_lat % ROW_TILE == 0
    depth = e_norm_g.shape[0] + o_norm_g.shape[0]

    cvec = jnp.zeros((8, d), F32).at[0].set(c[0]).at[1].set(c_ctx)
    rope_a = _rope_tables(n_lat, n_ctx, A_ROPE)
    rope_b = _rope_tables(n_lat, n_ctx, B_D)
    rope_c = _rope_tables(n_lat, n_ctx, C_HD)

    x_lat, x_ctx = x[0], ctx[0]
    pending = None
    for layer in range(depth):
        i = layer // 2
        update_ctx = layer < depth - 1
        pf = dict(n_lat=n_lat, n_ctx=n_ctx, update_ctx=update_ctx)
        if layer % 2 == 0:
            lam_init = 0.8 - 0.6 * math.exp(-0.3 * layer)
            mod = _modulation(cvec, e_w_mod[i], e_b_mod[i])
            qta, ka, vta, gta, qtb, kb, vtb, gtb = _even_pre(
                x_lat, x_ctx, mod, e_norm_g[i], e_w_in[i], a_g_cq[i], a_w_uq[i],
                a_g_ckv[i], a_w_ukv[i], a_g_qn[i], a_g_kn[i], b_g_qn[i], b_g_kn[i],
                rope_a, rope_b)
            ya = _prefix_flash(_flash_gate_kernel, 1, qta, ka, vta, gta,
                               _score_bound(A_QK, a_g_qn[i], a_g_kn[i]),
                               lambda tq: [], n_heads=A_HEADS, kv_group=1,
                               name="flash_mla", **pf)
            lam_vecs = jnp.stack([b_lam_q1[i], b_lam_k1[i], b_lam_q2[i], b_lam_k2[i]]).astype(F32)
            yb = _prefix_flash(
                functools.partial(_flash_diff_kernel, lam_init), 2, qtb, kb, vtb, gtb,
                _score_bound(B_D, b_g_qn[i], b_g_kn[i]),
                lambda tq: [lam_vecs, _bcast_rows(b_g_out[i], tq)],
                n_heads=B_HEADS, kv_group=1, name="flash_diff", **pf)
            ys, w_out = [ya, yb], e_w_out[i]
            if update_ctx:
                pending = (ys, w_out.T.astype(BF16), mod)
                continue
        else:
            mod = _modulation(cvec, o_w_mod[i], o_b_mod[i])
            outs = _odd_pre(x_lat, x_ctx, mod, o_norm_g[i], o_w_in[i], c_g_qn[i], c_g_kn[i],
                            rope_c, prev=pending)
            if pending is not None:
                x_lat, x_ctx, outs, pending = outs[0], outs[0][n_lat:], outs[1:], None
            qtc, kc, vtc, gtc = outs
            ys = [_prefix_flash(_flash_gate_kernel, 1, qtc, kc, vtc, gtc,
                                _score_bound(C_HD, c_g_qn[i], c_g_kn[i]),
                                lambda tq: [], n_heads=C_HEADS,
                                kv_group=C_HEADS // C_KV_HEADS, name="flash_gqa", **pf)]
            w_out = o_w_out[i]
        w_out_t = w_out.T.astype(BF16)
        if update_ctx:
            x_ctx = _post([y[1] for y in ys], w_out_t, x_ctx, mod, 1, "post_ctx")
        x_lat = _post([y[0] for y in ys], w_out_t, x_lat, mod, 0, "post", n_rows=n_lat)
    return x_lat.reshape(batch, n_lat, d)
```

```python
import functools
import math

import jax
import jax.numpy as jnp
import numpy as np
from jax import lax
from jax.experimental import pallas as pl
from jax.experimental.pallas import tpu as pltpu

F32 = jnp.float32
BF16 = jnp.bfloat16

GRID_W = 64
ROPE_THETA = 10000.0
EPS = 1e-6
LOG2E = math.log2(math.e)

A_HEADS, A_Q_LORA, A_KV_LORA, A_NOPE, A_ROPE, A_V = 8, 256, 128, 64, 32, 64
A_QK = A_NOPE + A_ROPE
B_HEADS, B_D = 4, 64
B_V = 2 * B_D
C_HEADS, C_KV_HEADS, C_HD = 8, 2, 128

HEAD_PAD = 128
ROW_TILE = 256
POST_TILE = 1024
FLASH_QUERY_LANES = 1024
BOUNDED_KEY_CHUNK = 3328
ONLINE_KEY_CHUNK = 1280
CHUNKS_PER_TRIP = 4
NEG_BIG = -1e30
SCORE_BOUND_LIMIT = 48.0
BOUND_MARGIN = 1.0 + 2.0 ** -6
VMEM_LIMIT_BYTES = 56 * 1024 * 1024


def _silu(v):
    return v / (1.0 + jnp.exp(-v))


def _dot(a, b):
    return jnp.dot(a, b, preferred_element_type=F32)


def _dot_nt(a, b):
    return lax.dot_general(a, b, (((1,), (1,)), ((), ())), preferred_element_type=F32)


def _rms_rows(v, g):
    ms = jnp.mean(v * v, axis=0, keepdims=True)
    return v * lax.rsqrt(ms + EPS) * g


def _rope_rows(v, cos, sin):
    half = v.shape[0] // 2
    x1, x2 = v[:half], v[half:]
    return jnp.concatenate([x1 * cos - x2 * sin, x2 * cos + x1 * sin], axis=0)


def _compiler_params(n_grid):
    return pltpu.CompilerParams(
        dimension_semantics=("arbitrary",) * n_grid,
        vmem_limit_bytes=VMEM_LIMIT_BYTES,
    )


def _mod_kernel(c_ref, w_ref, b_ref, o_ref):
    a = _silu(c_ref[...])
    w = w_ref[...]
    a_hi = a.astype(BF16)
    a_lo = (a - a_hi.astype(F32)).astype(BF16)
    w_hi = w.astype(BF16)
    w_lo = (w - w_hi.astype(F32)).astype(BF16)
    acc = _dot(a_hi, w_hi) + _dot(a_hi, w_lo) + _dot(a_lo, w_hi)
    o_ref[...] = acc + b_ref[...]


def _modulation(cvec, w_mod, b_mod):
    d, d3 = w_mod.shape
    col = 768
    return pl.pallas_call(
        _mod_kernel,
        grid=(d3 // col,),
        in_specs=[
            pl.BlockSpec((8, d), lambda j: (0, 0)),
            pl.BlockSpec((d, col), lambda j: (0, j)),
            pl.BlockSpec((1, col), lambda j: (0, j)),
        ],
        out_specs=pl.BlockSpec((8, col), lambda j: (0, j)),
        out_shape=jax.ShapeDtypeStruct((8, d3), F32),
        compiler_params=_compiler_params(1),
        name="modulation",
    )(cvec, w_mod, b_mod.reshape(1, d3))


def _token_specs(t, d, n_lat_tiles):
    return [pl.BlockSpec((t, d), lambda i: (jnp.minimum(i, n_lat_tiles - 1), 0)),
            pl.BlockSpec((t, d), lambda i: (0, 0))]


def _modulated_rows(x_ref, xc_ref, mod_ref, g_ref, is_ctx):
    return _modulate(jnp.where(is_ctx, xc_ref[...], x_ref[...]), mod_ref, g_ref, is_ctx)


def _modulate(x, mod_ref, g_ref, is_ctx):
    d = x.shape[-1]
    sh = jnp.where(is_ctx, mod_ref[1:2, 0:d], mod_ref[0:1, 0:d])
    sc = jnp.where(is_ctx, mod_ref[1:2, d:2 * d], mod_ref[0:1, d:2 * d])
    ms = jnp.mean(x * x, axis=-1, keepdims=True)
    y = x * lax.rsqrt(ms + EPS) * g_ref[...]
    return (y * (1.0 + sc) + sh).astype(BF16)


def _even_pre_kernel(n_lat_tiles,
                     x_ref, xc_ref, mod_ref, g_ref, wt_ref,
                     gcq_ref, wuq_ref, gckv_ref, wukv_ref, gqa_ref, gka_ref,
                     cosa_ref, sina_ref, gqb_ref, gkb_ref, cosb_ref, sinb_ref,
                     qta_ref, ka_ref, vta_ref, gta_ref,
                     qtb_ref, kb_ref, vtb_ref, gtb_ref):
    is_ctx = pl.program_id(0) >= n_lat_tiles
    hb = _modulated_rows(x_ref, xc_ref, mod_ref, g_ref, is_ctx)
    tok = hb.shape[0]

    o_cq, o_ckv, o_kr = 0, A_Q_LORA, A_Q_LORA + A_KV_LORA
    o_ga = o_kr + A_ROPE
    o_bq = o_ga + A_HEADS * A_V
    o_bk = o_bq + 2 * B_HEADS * B_D
    o_bv = o_bk + 2 * B_HEADS * B_D
    o_gb = o_bv + B_HEADS * B_V
    o_end = o_gb + B_HEADS * B_V

    def proj(lo, hi):
        return _dot_nt(wt_ref[lo:hi, :], hb)

    cq = _rms_rows(proj(o_cq, o_ckv), gcq_ref[...]).astype(BF16)
    q_all = _dot(wuq_ref[...], cq)
    cos_a, sin_a = cosa_ref[...], sina_ref[...]
    q_scale = (A_QK ** -0.5) * LOG2E
    pad_a = jnp.zeros((HEAD_PAD - A_QK, tok), F32)
    for h in range(A_HEADS):
        qh = _rms_rows(q_all[h * A_QK:(h + 1) * A_QK], gqa_ref[...])
        qh = jnp.concatenate(
            [qh[:A_NOPE], _rope_rows(qh[A_NOPE:], cos_a, sin_a), pad_a], axis=0)
        qta_ref[h] = (qh * q_scale).astype(BF16)

    ckv_kr = proj(o_ckv, o_ga)
    ckv = _rms_rows(ckv_kr[:A_KV_LORA], gckv_ref[...]).astype(BF16)
    kr = ckv_kr[A_KV_LORA:]
    kv_all = _dot(wukv_ref[...], ckv)
    for h in range(A_HEADS):
        base = h * (A_NOPE + A_V)
        kh = jnp.concatenate([kv_all[base:base + A_NOPE], kr], axis=0)
        kh = _rms_rows(kh, gka_ref[...])
        kh = jnp.concatenate(
            [kh[:A_NOPE], _rope_rows(kh[A_NOPE:], cos_a, sin_a), pad_a], axis=0)
        ka_ref[h] = jnp.transpose(kh).astype(BF16)
        vh = kv_all[base + A_NOPE:base + A_NOPE + A_V]
        vta_ref[h] = jnp.concatenate([vh, jnp.zeros_like(vh)], axis=0).astype(BF16)

    gta_ref[...] = _silu(proj(o_ga, o_bq))

    cos_b, sin_b = cosb_ref[...], sinb_ref[...]
    qb_scale = (B_D ** -0.5) * LOG2E
    pad_b = jnp.zeros((B_D, tok), F32)
    bq = proj(o_bq, o_bk)
    bk = proj(o_bk, o_bv)
    for h in range(B_HEADS):
        ks = []
        for j in range(2):
            m = 2 * h + j
            qm = _rms_rows(bq[m * B_D:(m + 1) * B_D], gqb_ref[...])
            qm = _rope_rows(qm, cos_b, sin_b) * qb_scale
            parts = [qm, pad_b] if j == 0 else [pad_b, qm]
            qtb_ref[m] = jnp.concatenate(parts, axis=0).astype(BF16)
            km = _rms_rows(bk[m * B_D:(m + 1) * B_D], gkb_ref[...])
            ks.append(_rope_rows(km, cos_b, sin_b))
        kb_ref[h] = jnp.transpose(jnp.concatenate(ks, axis=0)).astype(BF16)
    bv = proj(o_bv, o_gb)
    for h in range(B_HEADS):
        vtb_ref[h] = bv[h * B_V:(h + 1) * B_V].astype(BF16)
    gtb_ref[...] = _silu(proj(o_gb, o_end))


def _bcast_rows(g, tok):
    return jnp.broadcast_to(g.astype(F32)[:, None], (g.shape[0], tok))


def _even_pre(x_lat, x_ctx, mod, norm_g, w_in, a_g_cq, a_w_uq, a_g_ckv, a_w_ukv,
              a_g_qn, a_g_kn, b_g_qn, b_g_kn, rope_a, rope_b):
    n_lat, d = x_lat.shape
    t = ROW_TILE
    seq = n_lat + x_ctx.shape[0]
    n_tiles = seq // t
    e_in = w_in.shape[1]
    const2 = lambda i: (0, 0)
    tok2 = lambda i: (0, i)
    tok3 = lambda i: (0, 0, i)
    row3 = lambda i: (0, i, 0)
    in_specs = _token_specs(t, d, n_lat // t) + [
        pl.BlockSpec((8, 3 * d), const2),
        pl.BlockSpec((1, d), const2),
        pl.BlockSpec((e_in, d), const2),
        pl.BlockSpec((A_Q_LORA, t), const2),
        pl.BlockSpec((A_HEADS * A_QK, A_Q_LORA), const2),
        pl.BlockSpec((A_KV_LORA, t), const2),
        pl.BlockSpec((A_HEADS * (A_NOPE + A_V), A_KV_LORA), const2),
        pl.BlockSpec((A_QK, t), const2),
        pl.BlockSpec((A_QK, t), const2),
        pl.BlockSpec((A_ROPE // 2, t), tok2),
        pl.BlockSpec((A_ROPE // 2, t), tok2),
        pl.BlockSpec((B_D, t), const2),
        pl.BlockSpec((B_D, t), const2),
        pl.BlockSpec((B_D // 2, t), tok2),
        pl.BlockSpec((B_D // 2, t), tok2),
    ]
    out_shape = [
        jax.ShapeDtypeStruct((A_HEADS, HEAD_PAD, seq), BF16),
        jax.ShapeDtypeStruct((A_HEADS, seq, HEAD_PAD), BF16),
        jax.ShapeDtypeStruct((A_HEADS, 2 * A_V, seq), BF16),
        jax.ShapeDtypeStruct((A_HEADS * A_V, seq), F32),
        jax.ShapeDtypeStruct((2 * B_HEADS, HEAD_PAD, seq), BF16),
        jax.ShapeDtypeStruct((B_HEADS, seq, HEAD_PAD), BF16),
        jax.ShapeDtypeStruct((B_HEADS, B_V, seq), BF16),
        jax.ShapeDtypeStruct((B_HEADS * B_V, seq), F32),
    ]
    out_specs = [
        pl.BlockSpec((A_HEADS, HEAD_PAD, t), tok3),
        pl.BlockSpec((A_HEADS, t, HEAD_PAD), row3),
        pl.BlockSpec((A_HEADS, 2 * A_V, t), tok3),
        pl.BlockSpec((A_HEADS * A_V, t), tok2),
        pl.BlockSpec((2 * B_HEADS, HEAD_PAD, t), tok3),
        pl.BlockSpec((B_HEADS, t, HEAD_PAD), row3),
        pl.BlockSpec((B_HEADS, B_V, t), tok3),
        pl.BlockSpec((B_HEADS * B_V, t), tok2),
    ]
    return pl.pallas_call(
        functools.partial(_even_pre_kernel, n_lat // t),
        grid=(n_tiles,),
        in_specs=in_specs,
        out_specs=out_specs,
        out_shape=out_shape,
        compiler_params=_compiler_params(1),
        name="even_pre",
    )(x_lat, x_ctx, mod, norm_g.reshape(1, d), w_in.T.astype(BF16),
      _bcast_rows(a_g_cq, t), a_w_uq.T.astype(BF16),
      _bcast_rows(a_g_ckv, t), a_w_ukv.T.astype(BF16),
      _bcast_rows(a_g_qn, t), _bcast_rows(a_g_kn, t), rope_a[0], rope_a[1],
      _bcast_rows(b_g_qn, t), _bcast_rows(b_g_kn, t), rope_b[0], rope_b[1])


def _odd_pre_kernel(n_lat_tiles,
                    x_ref, xc_ref, mod_ref, g_ref, wt_ref, gq_ref, gk_ref, cos_ref, sin_ref,
                    qt_ref, k_ref, vt_ref, gt_ref):
    is_ctx = pl.program_id(0) >= n_lat_tiles
    hb = _modulated_rows(x_ref, xc_ref, mod_ref, g_ref, is_ctx)
    _odd_heads(hb, wt_ref, gq_ref, gk_ref, cos_ref, sin_ref, qt_ref, k_ref, vt_ref, gt_ref)


def _odd_heads(hb, wt_ref, gq_ref, gk_ref, cos_ref, sin_ref, qt_ref, k_ref, vt_ref, gt_ref):
    cos, sin = cos_ref[...], sin_ref[...]
    o_k = C_HEADS * C_HD
    o_v = o_k + C_KV_HEADS * C_HD
    o_g = o_v + C_KV_HEADS * C_HD
    o_end = o_g + C_HEADS * C_HD
    q_scale = (C_HD ** -0.5) * LOG2E
    q = _dot_nt(wt_ref[0:o_k, :], hb)
    for h in range(C_HEADS):
        qh = _rms_rows(q[h * C_HD:(h + 1) * C_HD], gq_ref[...])
        qt_ref[h] = (_rope_rows(qh, cos, sin) * q_scale).astype(BF16)
    kv = _dot_nt(wt_ref[o_k:o_g, :], hb)
    for h in range(C_KV_HEADS):
        kh = _rms_rows(kv[h * C_HD:(h + 1) * C_HD], gk_ref[...])
        k_ref[h] = jnp.transpose(_rope_rows(kh, cos, sin)).astype(BF16)
        vt_ref[h] = kv[(C_KV_HEADS + h) * C_HD:(C_KV_HEADS + h + 1) * C_HD].astype(BF16)
    gt_ref[...] = _silu(_dot_nt(wt_ref[o_g:o_end, :], hb))


def _post_odd_pre_kernel(n_lat_tiles, n_y, *refs):
    yl_refs, yc_refs = refs[:n_y], refs[n_y:2 * n_y]
    (wo_ref, x_ref, xc_ref, mod_prev_ref, mod_ref, g_ref, wt_ref, gq_ref, gk_ref,
     cos_ref, sin_ref, x_out_ref, qt_ref, k_ref, vt_ref, gt_ref, x_scr) = refs[2 * n_y:]
    d = x_ref.shape[-1]
    i = pl.program_id(0)

    @pl.when(i == 0)
    def _():
        x_scr[1] = jnp.zeros(x_scr.shape[1:], F32)

    hb = _modulate(x_scr[(i + 1) % 2], mod_ref, g_ref, i - 1 >= n_lat_tiles)
    _odd_heads(hb, wt_ref, gq_ref, gk_ref, cos_ref, sin_ref, qt_ref, k_ref, vt_ref, gt_ref)

    is_ctx = i >= n_lat_tiles
    acc = None
    off = 0
    for yl_ref, yc_ref in zip(yl_refs, yc_refs):
        w = yl_ref.shape[0]
        y = jnp.where(is_ctx, yc_ref[...], yl_ref[...])
        part = _dot(wo_ref[:, off:off + w], y)
        acc = part if acc is None else acc + part
        off += w
    gate = jnp.where(is_ctx, mod_prev_ref[1:2, 2 * d:3 * d], mod_prev_ref[0:1, 2 * d:3 * d])
    x = jnp.where(is_ctx, xc_ref[...], x_ref[...]) + gate * jnp.transpose(acc)
    x_out_ref[...] = x
    x_scr[i % 2] = x


def _odd_pre(x_lat, x_ctx, mod, norm_g, w_in, g_qn, g_kn, rope_c, prev=None):
    n_lat, d = x_lat.shape
    t = ROW_TILE
    seq = n_lat + x_ctx.shape[0]
    o_in = w_in.shape[1]
    n_lat_tiles = n_lat // t
    n_tiles = seq // t
    lag = 0 if prev is None else 1

    def tile(i):
        return jnp.maximum(i - lag, 0)

    const2 = lambda i: (0, 0)
    tok2 = lambda i: (0, tile(i))
    tok3 = lambda i: (0, 0, tile(i))
    in_specs = _token_specs(t, d, n_lat_tiles) + [
        pl.BlockSpec((8, 3 * d), const2),
        pl.BlockSpec((1, d), const2),
        pl.BlockSpec((o_in, d), const2),
        pl.BlockSpec((C_HD, t), const2),
        pl.BlockSpec((C_HD, t), const2),
        pl.BlockSpec((C_HD // 2, t), tok2),
        pl.BlockSpec((C_HD // 2, t), tok2),
    ]
    args = [x_lat, x_ctx, mod, norm_g.reshape(1, d), w_in.T.astype(BF16),
            _bcast_rows(g_qn, t), _bcast_rows(g_kn, t), rope_c[0], rope_c[1]]
    kernel_fn = functools.partial(_odd_pre_kernel, n_lat_tiles)
    name = "odd_pre"
    if prev is not None:
        ys, w_out_t, mod_prev = prev
        y_specs = ([pl.BlockSpec((y[0].shape[0], t),
                                 lambda i: (0, jnp.minimum(i, n_lat_tiles - 1))) for y in ys]
                   + [pl.BlockSpec((y[1].shape[0], t), const2) for y in ys])
        in_specs = (y_specs + [pl.BlockSpec(w_out_t.shape, const2)] + in_specs[:2]
                    + [pl.BlockSpec((8, 3 * d), const2)] + in_specs[2:])
        args = ([y[0] for y in ys] + [y[1] for y in ys] + [w_out_t] + args[:2]
                + [mod_prev] + args[2:])
        kernel_fn = functools.partial(_post_odd_pre_kernel, n_lat_tiles, len(ys))
        name = "post_odd_pre"
    out_shape = [
        jax.ShapeDtypeStruct((C_HEADS, C_HD, seq), BF16),
        jax.ShapeDtypeStruct((C_KV_HEADS, seq, C_HD), BF16),
        jax.ShapeDtypeStruct((C_KV_HEADS, C_HD, seq), BF16),
        jax.ShapeDtypeStruct((C_HEADS * C_HD, seq), F32),
    ]
    out_specs = [
        pl.BlockSpec((C_HEADS, C_HD, t), tok3),
        pl.BlockSpec((C_KV_HEADS, t, C_HD), lambda i: (0, tile(i), 0)),
        pl.BlockSpec((C_KV_HEADS, C_HD, t), tok3),
        pl.BlockSpec((C_HEADS * C_HD, t), tok2),
    ]
    scratch = []
    if prev is not None:
        out_shape = [jax.ShapeDtypeStruct((seq, d), F32)] + out_shape
        out_specs = [pl.BlockSpec((t, d), lambda i: (jnp.minimum(i, n_tiles - 1), 0))] + out_specs
        scratch = [pltpu.VMEM((2, t, d), F32)]
    return pl.pallas_call(
        kernel_fn,
        grid=(n_tiles + lag,),
        in_specs=in_specs,
        out_specs=out_specs,
        out_shape=out_shape,
        scratch_shapes=scratch,
        compiler_params=_compiler_params(1),
        name=name,
    )(*args)


def _flash_streams(qts, k_ref, vt_ref, s_ref, acc_ref, n_chunks, tk):
    n_s = len(qts)
    tq = qts[0].shape[1]
    for st in range(n_s):
        acc_ref[st] = jnp.zeros(acc_ref.shape[1:], F32)

    def scores(slot, c):
        start = pl.multiple_of(c * tk, tk)
        k = k_ref[0, pl.ds(start, tk), :]
        maxes = []
        for st in range(n_s):
            s = _dot(k, qts[st])
            s_ref[st, slot] = s
            maxes.append(jnp.max(s, axis=0, keepdims=True))
        return tuple(maxes)

    def consume(slot, c, state, maxes):
        ms, ls = state
        start = pl.multiple_of(c * tk, tk)
        vt = vt_ref[0, :, pl.ds(start, tk)]
        new_ms, new_ls = [], []
        for st in range(n_s):
            m_new = jnp.maximum(ms[st], maxes[st])
            alpha = jnp.exp2(ms[st] - m_new)
            p = jnp.exp2(s_ref[st, slot] - m_new)
            new_ls.append(alpha * ls[st] + jnp.sum(p, axis=0, keepdims=True))
            acc_ref[st] = alpha * acc_ref[st] + _dot(vt, p.astype(BF16))
            new_ms.append(m_new)
        return tuple(new_ms), tuple(new_ls)

    def step(c, slot, state, mx_cur):
        mx_next = scores(1 - slot, c + 1)
        return consume(slot, c, state, mx_cur), mx_next

    def group(j, carry):
        state, mx = carry
        for u in range(CHUNKS_PER_TRIP):
            state, mx = step(j * CHUNKS_PER_TRIP + u, u % 2, state, mx)
        return state, mx

    state = (tuple(jnp.full((1, tq), NEG_BIG, F32) for _ in range(n_s)),
             tuple(jnp.zeros((1, tq), F32) for _ in range(n_s)))
    mx = scores(0, 0)
    n_trips = (n_chunks - 1) // CHUNKS_PER_TRIP
    state, mx = lax.fori_loop(0, n_trips, group, (state, mx))
    for c in range(n_trips * CHUNKS_PER_TRIP, n_chunks - 1):
        state, mx = step(c, c % 2, state, mx)
    state = consume((n_chunks - 1) % 2, n_chunks - 1, state, mx)
    _, ls = state
    return [acc_ref[st] / ls[st] for st in range(n_s)]


def _bounded_streams(qts, b_all, k_ref, vt_ref, acc_ref, n_chunks, tk):
    n_s = len(qts)
    tq = qts[0].shape[1]
    q_all = jnp.concatenate(qts, axis=1) if n_s > 1 else qts[0]
    l_all = None
    for c in range(n_chunks):
        k = k_ref[0, c * tk:(c + 1) * tk, :]
        vt = vt_ref[0, :, c * tk:(c + 1) * tk]
        p = jnp.exp2(_dot(k, q_all) - b_all)
        p_sum = jnp.sum(p, axis=0, keepdims=True)
        pv = _dot(vt, p.astype(BF16))
        if c == 0:
            l_all = p_sum
            acc_ref[...] = pv
        else:
            l_all = l_all + p_sum
            acc_ref[...] += pv
    o_all = acc_ref[...] / l_all
    return [o_all[:, st * tq:(st + 1) * tq] for st in range(n_s)]


def _attend(bounded, n_s, n_chunks, tk, refs):
    qts = [refs[0][0, st] for st in range(n_s)]
    pos = 1
    if bounded:
        b_all = refs[pos][...]
        pos += 1
    k_ref, vt_ref = refs[pos:pos + 2]
    pos += 2
    if bounded:
        yt_ref, acc_ref = refs[-2:]
        outs = _bounded_streams(qts, b_all, k_ref, vt_ref, acc_ref, n_chunks, tk)
        return outs, refs[pos:-2], yt_ref
    yt_ref, s_ref, acc_ref = refs[-3:]
    outs = _flash_streams(qts, k_ref, vt_ref, s_ref, acc_ref, n_chunks, tk)
    return outs, refs[pos:-3], yt_ref


def _flash_gate_kernel(bounded, n_chunks, tk, *refs):
    (o,), (gate_ref,), yt_ref = _attend(bounded, 1, n_chunks, tk, refs)
    yt_ref[...] = (o[:gate_ref.shape[0]] * gate_ref[...]).astype(BF16)


def _flash_diff_kernel(lam_init, bounded, n_chunks, tk, *refs):
    (o1, o2), (gate_ref, lam_ref, gout_ref), yt_ref = _attend(bounded, 2, n_chunks, tk, refs)
    lv = lam_ref[...]
    lam = (jnp.exp(jnp.sum(lv[0:1] * lv[1:2], axis=-1, keepdims=True))
           - jnp.exp(jnp.sum(lv[2:3] * lv[3:4], axis=-1, keepdims=True)) + lam_init)
    o = o1 - lam * o2
    o = _rms_rows(o, gout_ref[...]) * (1.0 - lam_init)
    yt_ref[...] = (o * gate_ref[...]).astype(BF16)


def _flash_call(kernel_fn, n_streams, qt, bound, k, vt, gate, extra, *, n_heads, kv_group,
                q_tiles, q_tile0, tq, k_len, k_blk0, tk, name):
    dv = vt.shape[1]
    dv_out = gate.shape[0] // n_heads
    n_chunks = k_len // tk
    bounded = bound is not None
    q_specs = [pl.BlockSpec((1, n_streams, HEAD_PAD, tq), lambda h, i: (h, 0, 0, q_tile0 + i))]
    bound_specs = [pl.BlockSpec(bound.shape, lambda h, i: (0, 0))] if bounded else []
    in_specs = q_specs + bound_specs + [
        pl.BlockSpec((1, k_len, HEAD_PAD), lambda h, i: (h // kv_group, k_blk0, 0)),
        pl.BlockSpec((1, dv, k_len), lambda h, i: (h // kv_group, 0, k_blk0)),
        pl.BlockSpec((dv_out, tq), lambda h, i: (h, q_tile0 + i)),
    ] + [pl.BlockSpec(e.shape, lambda h, i: (0, 0)) for e in extra]
    if bounded:
        scratch = [pltpu.VMEM((dv, n_streams * tq), F32)]
    else:
        scratch = [pltpu.VMEM((n_streams, 2, tk, tq), F32),
                   pltpu.VMEM((n_streams, dv, tq), F32)]
    q4 = qt.reshape(n_heads, n_streams, HEAD_PAD, qt.shape[-1])
    args = [q4] + ([bound] if bounded else [])
    return pl.pallas_call(
        functools.partial(kernel_fn, bounded, n_chunks, tk),
        grid=(n_heads, q_tiles),
        in_specs=in_specs,
        out_specs=pl.BlockSpec((dv_out, tq), lambda h, i: (h, i)),
        out_shape=jax.ShapeDtypeStruct((n_heads * dv_out, q_tiles * tq), BF16),
        scratch_shapes=scratch,
        compiler_params=_compiler_params(2),
        name=name,
    )(*args, k, vt, gate, *extra)


def _score_bound(dim, g_q, g_k):
    return (math.sqrt(dim) * LOG2E * BOUND_MARGIN
            * jnp.max(jnp.abs(g_q)) * jnp.max(jnp.abs(g_k))).astype(F32)


def _prefix_flash(kernel_fn, n_streams, qt, k, vt, gate, score_bound, extra, *, n_heads,
                  kv_group, n_lat, n_ctx, update_ctx, name):
    seq = n_lat + n_ctx
    tq = FLASH_QUERY_LANES // n_streams
    if n_lat % tq != 0:
        tq = ROW_TILE
    common = dict(n_heads=n_heads, kv_group=kv_group)
    lat = dict(q_tiles=n_lat // tq, q_tile0=0, tq=tq, k_len=seq, k_blk0=0, **common)

    def key_chunk(size):
        return size if seq % size == 0 else ROW_TILE

    bound = jnp.full((1, n_streams * tq), score_bound, F32)
    ex = extra(tq)

    def bounded(*a):
        return _flash_call(kernel_fn, n_streams, a[0], a[1], *a[2:5], a[5:], name=name,
                           tk=key_chunk(BOUNDED_KEY_CHUNK), **lat)

    def online(*a):
        return _flash_call(kernel_fn, n_streams, a[0], None, *a[2:5], a[5:],
                           name=name + "_online", tk=key_chunk(ONLINE_KEY_CHUNK), **lat)

    y = lax.cond(score_bound <= SCORE_BOUND_LIMIT, bounded, online,
                 qt, bound, k, vt, gate, *ex)
    if not update_ctx:
        return y, None
    y_ctx = _flash_call(kernel_fn, n_streams, qt, None, k, vt, gate, extra(n_ctx),
                        q_tiles=1, q_tile0=n_lat // n_ctx, tq=n_ctx,
                        k_len=n_ctx, k_blk0=n_lat // n_ctx, tk=n_ctx,
                        name=name + "_ctx", **common)
    return y, y_ctx


def _post_kernel(mod_row, n_y, *refs):
    y_refs = refs[:n_y]
    wo_ref, x_ref, mod_ref, o_ref = refs[n_y:]
    d = x_ref.shape[-1]
    acc = None
    off = 0
    for y_ref in y_refs:
        w = y_ref.shape[0]
        part = _dot(wo_ref[:, off:off + w], y_ref[...])
        acc = part if acc is None else acc + part
        off += w
    gate = mod_ref[mod_row:mod_row + 1, 2 * d:3 * d]
    o_ref[...] = x_ref[...] + gate * jnp.transpose(acc)


def _post(ys, w_out_t, x, mod, mod_row, name, n_rows=None):
    d = x.shape[1]
    n_rows = x.shape[0] if n_rows is None else n_rows
    t = POST_TILE if n_rows % POST_TILE == 0 else ROW_TILE
    width = w_out_t.shape[1]
    in_specs = [pl.BlockSpec((y.shape[0], t), lambda i: (0, i)) for y in ys] + [
        pl.BlockSpec((d, width), lambda i: (0, 0)),
        pl.BlockSpec((t, d), lambda i: (i, 0)),
        pl.BlockSpec((8, 3 * d), lambda i: (0, 0)),
    ]
    return pl.pallas_call(
        functools.partial(_post_kernel, mod_row, len(ys)),
        grid=(n_rows // t,),
        in_specs=in_specs,
        out_specs=pl.BlockSpec((t, d), lambda i: (i, 0)),
        out_shape=jax.ShapeDtypeStruct((n_rows, d), F32),
        compiler_params=_compiler_params(1),
        name=name,
    )(*ys, w_out_t, x, mod)


def _rope_tables(n_lat, n_ctx, dim):
    tok = np.arange(n_lat)
    rows, cols = tok // GRID_W, tok % GRID_W
    n_freq = dim // 4
    inv = ROPE_THETA ** (-np.arange(n_freq) / n_freq)
    ang = np.concatenate([rows[:, None] * inv, cols[:, None] * inv], axis=-1)
    ang = np.concatenate([ang, np.zeros((n_ctx, dim // 2))], axis=0)
    return (jnp.asarray(np.cos(ang).T, dtype=F32), jnp.asarray(np.sin(ang).T, dtype=F32))


def kernel(x, c, ctx, c_ctx, e_norm_g, e_w_mod, e_b_mod, e_w_in, a_g_cq, a_w_uq, a_g_ckv, a_w_ukv, a_g_qn, a_g_kn, b_g_qn, b_g_kn, b_lam_q1, b_lam_k1, b_lam_q2, b_lam_k2, b_g_out, e_w_out, o_norm_g, o_w_mod, o_b_mod, o_w_in, c_g_qn, c_g_kn, o_w_out):
    batch, n_lat, d = x.shape
    n_ctx = ctx.shape[1]
    assert batch == 1 and n_ctx == ROW_TILE and n_lat % ROW_TILE == 0
    depth = e_norm_g.shape[0] + o_norm_g.shape[0]

    cvec = jnp.zeros((8, d), F32).at[0].set(c[0]).at[1].set(c_ctx)
    rope_a = _rope_tables(n_lat, n_ctx, A_ROPE)
    rope_b = _rope_tables(n_lat, n_ctx, B_D)
    rope_c = _rope_tables(n_lat, n_ctx, C_HD)

    x_lat, x_ctx = x[0], ctx[0]
    pending = None
    for layer in range(depth):
        i = layer // 2
        update_ctx = layer < depth - 1
        pf = dict(n_lat=n_lat, n_ctx=n_ctx, update_ctx=update_ctx)
        if layer % 2 == 0:
            lam_init = 0.8 - 0.6 * math.exp(-0.3 * layer)
            mod = _modulation(cvec, e_w_mod[i], e_b_mod[i])
            qta, ka, vta, gta, qtb, kb, vtb, gtb = _even_pre(
                x_lat, x_ctx, mod, e_norm_g[i], e_w_in[i], a_g_cq[i], a_w_uq[i],
                a_g_ckv[i], a_w_ukv[i], a_g_qn[i], a_g_kn[i], b_g_qn[i], b_g_kn[i],
                rope_a, rope_b)
            ya = _prefix_flash(_flash_gate_kernel, 1, qta, ka, vta, gta,
                               _score_bound(A_QK, a_g_qn[i], a_g_kn[i]),
                               lambda tq: [], n_heads=A_HEADS, kv_group=1,
                               name="flash_mla", **pf)
            lam_vecs = jnp.stack([b_lam_q1[i], b_lam_k1[i], b_lam_q2[i], b_lam_k2[i]]).astype(F32)
            yb = _prefix_flash(
                functools.partial(_flash_diff_kernel, lam_init), 2, qtb, kb, vtb, gtb,
                _score_bound(B_D, b_g_qn[i], b_g_kn[i]),
                lambda tq: [lam_vecs, _bcast_rows(b_g_out[i], tq)],
                n_heads=B_HEADS, kv_group=1, name="flash_diff", **pf)
            ys, w_out = [ya, yb], e_w_out[i]
            if update_ctx:
                pending = (ys, w_out.T.astype(BF16), mod)
                continue
        else:
            mod = _modulation(cvec, o_w_mod[i], o_b_mod[i])
            outs = _odd_pre(x_lat, x_ctx, mod, o_norm_g[i], o_w_in[i], c_g_qn[i], c_g_kn[i],
                            rope_c, prev=pending)
            if pending is not None:
                x_lat, x_ctx, outs, pending = outs[0], outs[0][n_lat:], outs[1:], None
            qtc, kc, vtc, gtc = outs
            ys = [_prefix_flash(_flash_gate_kernel, 1, qtc, kc, vtc, gtc,
                                _score_bound(C_HD, c_g_qn[i], c_g_kn[i]),
                                lambda tq: [], n_heads=C_HEADS,
                                kv_group=C_HEADS // C_KV_HEADS, name="flash_gqa", **pf)]
            w_out = o_w_out[i]
        w_out_t = w_out.T.astype(BF16)
        if update_ctx:
            x_ctx = _post([y[1] for y in ys], w_out_t, x_ctx, mod, 1, "post_ctx")
        x_lat = _post([y[0] for y in ys], w_out_t, x_lat, mod, 0, "post", n_rows=n_lat)
    return x_lat.reshape(batch, n_lat, d)
```
